```python
import math
import jax, jax.numpy as jnp
from jax import lax
import numpy as np

D_MODEL = 2048
BATCH = 2
SEQ = 8192
DEPTH = 1

RET_HEADS = 4
RET_QK_DIM = 256
RET_V_DIM = 512
RET_CHUNK = 128
RET_N_SCALES = 2 * RET_HEADS
ROPE_BASE = 10000.0
DN_K_HEADS = 8
DN_V_HEADS = 16
DN_K_DIM = 128
DN_V_DIM = 128
DN_CONV = 4
DN_CHUNK = 64
N_EXPERTS = 32
TOP_K = 4
D_FF = 2048
SWIGLU_LIMIT = 7.0
SWIGLU_ALPHA = 1.702
MOE_BLOCK = 256
EPS = 1e-6

RET_QK = RET_HEADS * RET_QK_DIM
RET_V = RET_HEADS * RET_V_DIM
DN_QK = DN_K_HEADS * DN_K_DIM
DN_V = DN_V_HEADS * DN_V_DIM
DN_CONV_CH = 2 * DN_QK + DN_V
SPLITS = (RET_QK, RET_QK, RET_V, RET_V, DN_CONV_CH, DN_V, DN_V_HEADS, DN_V_HEADS, DN_V_HEADS, DN_V_HEADS, D_MODEL, D_MODEL)
D_IN = sum(SPLITS)

kernel_name = "hybrid_retention_gdn_moe_block"


def rms_norm(t, w):
    t32 = t.astype(jnp.float32)
    out = t32 * lax.rsqrt(jnp.mean(t32 * t32, axis=-1, keepdims=True) + EPS) * w.astype(jnp.float32)
    return out.astype(t.dtype)


def to_heads(t, n):
    b, l, _ = t.shape
    return t.reshape(b, l, n, -1).transpose(0, 2, 1, 3)


def from_heads(t):
    b, n, l, d = t.shape
    return t.transpose(0, 2, 1, 3).reshape(b, l, n * d)


def flip_time(t):
    return jnp.flip(t, axis=2)


def rotary_tables(length, dim):
    angle = 1.0 / (ROPE_BASE ** jnp.linspace(0.0, 1.0, dim // 2, dtype=jnp.float32))
    angle = jnp.repeat(angle, 2)
    theta = jnp.arange(length, dtype=jnp.float32)[:, None] * angle[None, :]
    return jnp.cos(theta), jnp.sin(theta)


def rotate_every_two(t):
    t1 = t[..., 0::2]
    t2 = t[..., 1::2]
    return jnp.stack((-t2, t1), axis=-1).reshape(t.shape)


def retention_chunkwise(q, k, v, log_g, include_diag):
    b, h, l, dk = q.shape
    dv = v.shape[-1]
    c = RET_CHUNK
    n = l // c
    q = q.reshape(b, h, n, c, dk)
    k = k.reshape(b, h, n, c, dk)
    v = v.reshape(b, h, n, c, dv)
    idx = jnp.arange(c, dtype=jnp.float32)
    diff = idx[:, None] - idx[None, :]
    keep = (diff >= 0) if include_diag else (diff > 0)
    dmask = jnp.where(keep[None], jnp.exp(jnp.where(keep, diff, 0.0)[None] * log_g[:, None, None]), 0.0)
    scores = jnp.einsum('bhnid,bhnjd->bhnij', q, k) * dmask[None, :, None]
    intra = jnp.einsum('bhnij,bhnje->bhnie', scores, v)
    q_dec = jnp.exp((idx + 1.0)[None, :] * log_g[:, None])
    k_dec = jnp.exp((c - 1.0 - idx)[None, :] * log_g[:, None])
    c_dec = jnp.exp(c * log_g)[:, None, None]
    qd = q * q_dec[None, :, None, :, None]
    kd = k * k_dec[None, :, None, :, None]

    def step(state, xs):
        q_n, k_n, v_n = xs
        o = jnp.einsum('bhid,bhde->bhie', q_n, state)
        state = state * c_dec + jnp.einsum('bhjd,bhje->bhde', k_n, v_n)
        return state, o

    xs = (jnp.moveaxis(qd, 2, 0), jnp.moveaxis(kd, 2, 0), jnp.moveaxis(v, 2, 0))
    s0 = jnp.zeros((b, h, dk, dv), q.dtype)
    _, inter = lax.scan(step, s0, xs)
    return (intra + jnp.moveaxis(inter, 0, 2)).reshape(b, h, l, dv)


def gated_delta_chunkwise(q, k, v, beta, g):
    b, h, l, dk = q.shape
    dv = v.shape[-1]
    c = DN_CHUNK
    n = l // c
    q = q.reshape(b, h, n, c, dk)
    k = k.reshape(b, h, n, c, dk)
    v = v.reshape(b, h, n, c, dv)
    beta = beta.reshape(b, h, n, c)
    gc = jnp.cumsum(g.reshape(b, h, n, c), axis=-1)
    idx = jnp.arange(c)
    lower_incl = idx[:, None] >= idx[None, :]
    lower_strict = idx[:, None] > idx[None, :]
    decay = jnp.exp(jnp.where(lower_incl, gc[..., :, None] - gc[..., None, :], -jnp.inf))
    kb = k * beta[..., None]
    m = jnp.where(lower_strict, jnp.einsum('bhnid,bhnjd->bhnij', kb, k) * decay, 0.0)
    eye = jnp.eye(c, dtype=q.dtype)
    tinv = lax.linalg.triangular_solve(m + eye, jnp.broadcast_to(eye, m.shape), left_side=True, lower=True)
    u = jnp.einsum('bhnij,bhnje->bhnie', tinv, v * beta[..., None])
    w = jnp.einsum('bhnij,bhnjd->bhnid', tinv, kb * jnp.exp(gc)[..., None])
    qk = jnp.einsum('bhnid,bhnjd->bhnij', q, k) * decay

    def step(state, xs):
        q_n, k_n, u_n, w_n, qk_n, g_n = xs
        v_new = u_n - jnp.einsum('bhcd,bhde->bhce', w_n, state)
        o = (jnp.einsum('bhcd,bhde->bhce', q_n * jnp.exp(g_n)[..., None], state)
             + jnp.einsum('bhij,bhje->bhie', qk_n, v_new))
        g_last = g_n[..., -1:]
        state = (state * jnp.exp(g_last)[..., None]
                 + jnp.einsum('bhcd,bhce->bhde', k_n * jnp.exp(g_last - g_n)[..., None], v_new))
        return state, o

    xs = tuple(jnp.moveaxis(t, 2, 0) for t in (q, k, u, w, qk, gc))
    s0 = jnp.zeros((b, h, dk, dv), q.dtype)
    _, o = lax.scan(step, s0, xs)
    return jnp.moveaxis(o, 0, 2).reshape(b, h, l, dv)


def centred_depthwise_conv(t, w):
    kw = w.shape[0]
    pad = ((kw - 1) - (kw - 1) // 2, (kw - 1) // 2)
    return lax.conv_general_dilated(t, w[:, None, :].astype(t.dtype), window_strides=(1,), padding=[pad],
                                    dimension_numbers=('NWC', 'WIO', 'NWC'), feature_group_count=t.shape[-1])


def retention_branch(rq, rk, rv, rg):
    length = rq.shape[1]
    f32 = jnp.float32
    q = to_heads(rq, RET_HEADS).astype(f32)
    k = to_heads(rk, RET_HEADS).astype(f32) * (RET_QK_DIM ** -0.5)
    v = to_heads(rv, RET_HEADS).astype(f32)
    cos, sin = rotary_tables(length, RET_QK_DIM)
    q = q * cos + rotate_every_two(q) * sin
    k = k * cos + rotate_every_two(k) * sin
    log_g = jnp.log(1.0 - 2.0 ** (-5.0 - jnp.arange(RET_N_SCALES, dtype=f32)))
    o_f = retention_chunkwise(q, k, v, log_g[0::2], True)
    o_b = flip_time(retention_chunkwise(flip_time(q), flip_time(k), flip_time(v), log_g[1::2], False))
    o = o_f + o_b
    o = o * lax.rsqrt(jnp.mean(o * o, axis=-1, keepdims=True) + EPS)
    o = from_heads(o)
    return (o * jax.nn.silu(rg.astype(f32))).astype(rg.dtype)


def deltanet_branch(dqkv, dz, db_f, db_b, da_f, da_b, conv_w, a_log_f, a_log_b, dt_f, dt_b, norm_w):
    f32 = jnp.float32
    qkv = jax.nn.silu(centred_depthwise_conv(dqkv, conv_w))
    dq, dk, dv = jnp.split(qkv, [DN_QK, 2 * DN_QK], axis=-1)
    l2 = lambda t: t * lax.rsqrt(jnp.sum(t * t, axis=-1, keepdims=True) + EPS)
    rep = DN_V_HEADS // DN_K_HEADS
    q = jnp.repeat(l2(to_heads(dq, DN_K_HEADS).astype(f32)) * (DN_K_DIM ** -0.5), rep, axis=1)
    k = jnp.repeat(l2(to_heads(dk, DN_K_HEADS).astype(f32)), rep, axis=1)
    v = to_heads(dv, DN_V_HEADS).astype(f32)

    def dir_gates(bp, ap, a_log, dt_bias):
        beta = jax.nn.sigmoid(bp.astype(f32)).transpose(0, 2, 1)
        g = (-jnp.exp(a_log.astype(f32)) * jax.nn.softplus(ap.astype(f32) + dt_bias.astype(f32))).transpose(0, 2, 1)
        return beta, g

    beta_f, g_f = dir_gates(db_f, da_f, a_log_f, dt_f)
    beta_b, g_b = dir_gates(db_b, da_b, a_log_b, dt_b)
    o_f = gated_delta_chunkwise(q, k, v, beta_f, g_f)
    o_b = flip_time(gated_delta_chunkwise(flip_time(q), flip_time(k), flip_time(v), flip_time(beta_b), flip_time(g_b)))
    o = o_f + o_b
    o = o * lax.rsqrt(jnp.mean(o * o, axis=-1, keepdims=True) + EPS) * norm_w.astype(f32)
    o = from_heads(o)
    return (o * jax.nn.silu(dz.astype(f32))).astype(dz.dtype)


def token_mixer(xn, w_in, conv_w, a_log_f, a_log_b, dt_f, dt_b, dn_norm_w, w_br, w_bd, w_out):
    proj = xn @ w_in
    offsets = [int(o) for o in np.cumsum(SPLITS)[:-1]]
    rq, rk, rv, rg, dqkv, dz, db_f, db_b, da_f, da_b, gate_r, gate_d = jnp.split(proj, offsets, axis=-1)
    ret = retention_branch(rq, rk, rv, rg)
    dn = deltanet_branch(dqkv, dz, db_f, db_b, da_f, da_b, conv_w, a_log_f, a_log_b, dt_f, dt_b, dn_norm_w)
    merged = jax.nn.sigmoid(gate_r) * (ret @ w_br) + jax.nn.sigmoid(gate_d) * (dn @ w_bd)
    return merged @ w_out


def moe_ffn(xn, w_router, b_router, w_gate, b_gate, w_up, b_up, w_down, b_down):
    b, l, d = xn.shape
    t = b * l
    xt = xn.reshape(t, d)
    logits = (xt @ w_router + b_router).astype(jnp.float32)
    top_logits, top_idx = lax.top_k(logits, TOP_K)
    top_w = jax.nn.softmax(top_logits, axis=-1).astype(xn.dtype)
    n_assign = t * TOP_K
    n_rows = ((n_assign + N_EXPERTS * (MOE_BLOCK - 1) + MOE_BLOCK - 1) // MOE_BLOCK) * MOE_BLOCK
    n_blocks = n_rows // MOE_BLOCK
    flat_e = top_idx.reshape(-1).astype(jnp.int32)
    flat_tok = jnp.repeat(jnp.arange(t, dtype=jnp.int32), TOP_K)
    flat_w = top_w.reshape(-1)
    order = jnp.argsort(flat_e, stable=True)
    sorted_e = flat_e[order]
    counts = jnp.bincount(flat_e, length=N_EXPERTS)
    starts = jnp.cumsum(counts) - counts
    padded = ((counts + MOE_BLOCK - 1) // MOE_BLOCK) * MOE_BLOCK
    pad_ends = jnp.cumsum(padded)
    pad_starts = pad_ends - padded
    dest = pad_starts[sorted_e] + (jnp.arange(n_assign, dtype=jnp.int32) - starts[sorted_e])
    buf_tok = jnp.full((n_rows,), t, jnp.int32).at[dest].set(flat_tok[order])
    buf_w = jnp.zeros((n_rows,), xn.dtype).at[dest].set(flat_w[order])
    block_e = jnp.clip(jnp.searchsorted(pad_ends, jnp.arange(n_blocks, dtype=jnp.int32) * MOE_BLOCK, side='right'),
                       0, N_EXPERTS - 1)
    xt_pad = jnp.concatenate([xt, jnp.zeros((1, d), xt.dtype)], axis=0)

    def expert_block(args):
        tok, e = args
        xb = xt_pad[tok]
        gate = xb @ w_gate[e] + b_gate[e]
        up = xb @ w_up[e] + b_up[e]
        gate = jnp.minimum(gate, SWIGLU_LIMIT)
        up = jnp.clip(up, -SWIGLU_LIMIT, SWIGLU_LIMIT)
        hid = (up + 1.0) * gate * jax.nn.sigmoid(SWIGLU_ALPHA * gate)
        return hid @ w_down[e] + b_down[e]

    yb = lax.map(expert_block, (buf_tok.reshape(n_blocks, MOE_BLOCK), block_e))
    y = jax.ops.segment_sum(yb.reshape(n_rows, d) * buf_w[:, None], buf_tok, num_segments=t + 1)[:t]
    return y.reshape(b, l, d)


def setup_inputs(seed: int = 0) -> dict:
    key = jax.random.key(seed)
    ks = jax.random.split(key, 24)
    f32 = jnp.float32
    nrm = lambda k, shape, s: jax.random.normal(k, shape, f32) * s
    dt_lo, dt_hi = math.log(0.001), math.log(0.1)

    def dt_bias(k):
        dt = jnp.exp(jax.random.uniform(k, (DEPTH, DN_V_HEADS), f32) * (dt_hi - dt_lo) + dt_lo)
        return dt + jnp.log(-jnp.expm1(-dt))

    return {
        "x": jax.random.normal(ks[0], (BATCH, SEQ, D_MODEL), f32),
        "norm1_w": 1.0 + nrm(ks[1], (DEPTH, D_MODEL), 0.02),
        "w_in": nrm(ks[2], (DEPTH, D_MODEL, D_IN), D_MODEL ** -0.5),
        "conv_w": nrm(ks[3], (DEPTH, DN_CONV, DN_CONV_CH), DN_CONV ** -0.5),
        "dn_a_log_f": jnp.log(jax.random.uniform(ks[4], (DEPTH, DN_V_HEADS), f32, 1.0, 16.0)),
        "dn_a_log_b": jnp.log(jax.random.uniform(ks[5], (DEPTH, DN_V_HEADS), f32, 1.0, 16.0)),
        "dn_dt_bias_f": dt_bias(ks[6]),
        "dn_dt_bias_b": dt_bias(ks[7]),
        "dn_norm_w": 1.0 + nrm(ks[8], (DEPTH, DN_V_DIM), 0.02),
        "w_branch_ret": nrm(ks[9], (DEPTH, RET_V, D_MODEL), RET_V ** -0.5),
        "w_branch_dn": nrm(ks[10], (DEPTH, DN_V, D_MODEL), DN_V ** -0.5),
        "w_out": nrm(ks[11], (DEPTH, D_MODEL, D_MODEL), D_MODEL ** -0.5),
        "norm2_w": 1.0 + nrm(ks[12], (DEPTH, D_MODEL), 0.02),
        "w_router": nrm(ks[13], (DEPTH, D_MODEL, N_EXPERTS), D_MODEL ** -0.5),
        "b_router": nrm(ks[14], (DEPTH, N_EXPERTS), 0.01),
        "w_gate": nrm(ks[15], (DEPTH, N_EXPERTS, D_MODEL, D_FF), D_MODEL ** -0.5),
        "b_gate": nrm(ks[16], (DEPTH, N_EXPERTS, D_FF), 0.02),
        "w_up": nrm(ks[17], (DEPTH, N_EXPERTS, D_MODEL, D_FF), D_MODEL ** -0.5),
        "b_up": nrm(ks[18], (DEPTH, N_EXPERTS, D_FF), 0.02),
        "w_down": nrm(ks[19], (DEPTH, N_EXPERTS, D_FF, D_MODEL), D_FF ** -0.5),
        "b_down": nrm(ks[20], (DEPTH, N_EXPERTS, D_MODEL), 0.02),
        "norm_f_w": 1.0 + nrm(ks[21], (D_MODEL,), 0.02),
    }


def reference(x, norm1_w, w_in, conv_w, dn_a_log_f, dn_a_log_b, dn_dt_bias_f, dn_dt_bias_b, dn_norm_w,
              w_branch_ret, w_branch_dn, w_out, norm2_w, w_router, b_router, w_gate, b_gate, w_up, b_up,
              w_down, b_down, norm_f_w):
    h = x
    for i in range(DEPTH):
        xn = rms_norm(h, norm1_w[i])
        h = h + token_mixer(xn, w_in[i], conv_w[i], dn_a_log_f[i], dn_a_log_b[i], dn_dt_bias_f[i], dn_dt_bias_b[i],
                            dn_norm_w[i], w_branch_ret[i], w_branch_dn[i], w_out[i])
        xn = rms_norm(h, norm2_w[i])
        h = h + moe_ffn(xn, w_router[i], b_router[i], w_gate[i], b_gate[i], w_up[i], b_up[i], w_down[i], b_down[i])
    return rms_norm(h, norm_f_w)
```

```python
import functools

import jax
import jax.numpy as jnp
from jax import lax
from jax.experimental import pallas as pl
from jax.experimental.pallas import tpu as pltpu

F32 = jnp.float32
BF16 = jnp.bfloat16

D_MODEL = 2048
RET_HEADS = 4
RET_QK_DIM = 256
RET_V_DIM = 512
ROPE_BASE = 10000.0
DN_K_HEADS = 8
DN_V_HEADS = 16
DN_DIM = 128
DN_CONV = 4
DN_CHUNK = 64
N_EXPERTS = 32
TOP_K = 4
D_FF = 2048
SWIGLU_LIMIT = 7.0
SWIGLU_ALPHA = 1.702
EPS = 1e-6

RET_QK = RET_HEADS * RET_QK_DIM
RET_V = RET_HEADS * RET_V_DIM
DN_QK = DN_K_HEADS * DN_DIM
DN_V = DN_V_HEADS * DN_DIM
DN_CONV_CH = 2 * DN_QK + DN_V

OFF_RQ = 0
OFF_RK = OFF_RQ + RET_QK
OFF_RV = OFF_RK + RET_QK
OFF_RG = OFF_RV + RET_V
OFF_DQKV = OFF_RG + RET_V
OFF_DZ = OFF_DQKV + DN_CONV_CH
OFF_GR = OFF_DZ + DN_V
OFF_GD = OFF_GR + D_MODEL
N_MAIN = OFF_GD + D_MODEL
LANES = 128

RET_CHUNK = 256
DN_TILE = 256
MOE_BLOCK = 256
VMEM_LIMIT = 56 * 1024 * 1024


def _cparams(sem):
    return pltpu.CompilerParams(dimension_semantics=sem, vmem_limit_bytes=VMEM_LIMIT)


def _dot(a, b):
    return jnp.dot(a, b, preferred_element_type=F32)


def _dot_nt(a, b):
    return lax.dot_general(a, b, (((1,), (1,)), ((), ())), preferred_element_type=F32)


def _dot_tn(a, b):
    return lax.dot_general(a, b, (((0,), (0,)), ((), ())), preferred_element_type=F32)


def _sigmoid(x):
    return 1.0 / (1.0 + jnp.exp(-x))


def _norm_matmul_kernel(x_ref, nw_ref, w_ref, o_ref, xn_ref):
    @pl.when(pl.program_id(1) == 0)
    def _():
        x = x_ref[...]
        ms = jnp.mean(x * x, axis=-1, keepdims=True)
        xn_ref[...] = (x * lax.rsqrt(ms + EPS) * nw_ref[...]).astype(BF16)

    o_ref[...] = _dot(xn_ref[...], w_ref[...]).astype(o_ref.dtype)


def _norm_matmul(x, nw, w, out_dtype, tm, tn):
    t, d = x.shape
    n = w.shape[1]
    return pl.pallas_call(
        _norm_matmul_kernel,
        grid=(t // tm, n // tn),
        in_specs=[
            pl.BlockSpec((tm, d), lambda i, j: (i, 0)),
            pl.BlockSpec((1, d), lambda i, j: (0, 0)),
            pl.BlockSpec((d, tn), lambda i, j: (0, j)),
        ],
        out_specs=pl.BlockSpec((tm, tn), lambda i, j: (i, j)),
        out_shape=jax.ShapeDtypeStruct((t, n), out_dtype),
        scratch_shapes=[pltpu.VMEM((tm, d), BF16)],
        compiler_params=_cparams(("parallel", "arbitrary")),
        name="norm_in_proj",
    )(x, nw, w)


def _gate_proj_kernel(x_ref, nw_ref, w_ref, alog_ref, dtb_ref, o_ref):
    x = x_ref[...]
    ms = jnp.mean(x * x, axis=-1, keepdims=True)
    xn = (x * lax.rsqrt(ms + EPS) * nw_ref[...]).astype(BF16)
    p = _dot(xn, w_ref[...])
    lane = lax.broadcasted_iota(jnp.int32, p.shape, 1)
    beta = _sigmoid(p)
    z = p + dtb_ref[...]
    softplus = jnp.maximum(z, 0.0) + jnp.log(1.0 + jnp.exp(-jnp.abs(z)))
    g = -jnp.exp(alog_ref[...]) * softplus
    o_ref[...] = jnp.where(lane < 2 * DN_V_HEADS, beta, jnp.where(lane < 4 * DN_V_HEADS, g, 0.0))


def _gate_proj(x, nw, w_small, alog_row, dtb_row, tm):
    t, d = x.shape
    return pl.pallas_call(
        _gate_proj_kernel,
        grid=(t // tm,),
        in_specs=[
            pl.BlockSpec((tm, d), lambda i: (i, 0)),
            pl.BlockSpec((1, d), lambda i: (0, 0)),
            pl.BlockSpec((d, LANES), lambda i: (0, 0)),
            pl.BlockSpec((1, LANES), lambda i: (0, 0)),
            pl.BlockSpec((1, LANES), lambda i: (0, 0)),
        ],
        out_specs=pl.BlockSpec((tm, LANES), lambda i: (i, 0)),
        out_shape=jax.ShapeDtypeStruct((t, LANES), F32),
        compiler_params=_cparams(("parallel",)),
        name="dn_gate_proj",
    )(x, nw, w_small, alog_row, dtb_row)


def _rotary(t_ref, cos, sin):
    t = t_ref[...].astype(F32)
    half = RET_QK_DIM // 2
    t1 = t[:, :half]
    t2 = t[:, half:]
    return jnp.concatenate([t1 * cos - t2 * sin, t2 * cos + t1 * sin], axis=1)


def _ret_fwd_kernel(lg_ref, q_ref, k_ref, v_ref, cos_ref, sin_ref, o_ref, state_ref, dmask_ref):
    c = RET_CHUNK
    h = pl.program_id(1)
    lg_f = lg_ref[2 * h]
    lg_b = lg_ref[2 * h + 1]

    @pl.when(pl.program_id(2) == 0)
    def _():
        state_ref[...] = jnp.zeros_like(state_ref)
        i = lax.broadcasted_iota(jnp.int32, (c, c), 0)
        j = lax.broadcasted_iota(jnp.int32, (c, c), 1)
        d = (i - j).astype(F32)
        dmask_ref[...] = jnp.where(d >= 0, jnp.exp(d * lg_f), jnp.exp(-d * lg_b))

    cos = cos_ref[...]
    sin = sin_ref[...]
    q = _rotary(q_ref, cos, sin)
    k = _rotary(k_ref, cos, sin) * (RET_QK_DIM ** -0.5)
    v = v_ref[...]
    idx = lax.broadcasted_iota(jnp.int32, (c, 1), 0).astype(F32)
    q_dec = jnp.exp((idx + 1.0) * lg_f)
    k_dec = jnp.exp((c - 1.0 - idx) * lg_f)
    c_dec = jnp.exp(jnp.full((1, 1), c, F32) * lg_f)
    qb = q.astype(BF16)
    kb = k.astype(BF16)
    scores = _dot_nt(qb, kb) * dmask_ref[...]
    state = state_ref[...]
    o = _dot(scores.astype(BF16), v) + _dot((q * q_dec).astype(BF16), state.astype(BF16))
    o_ref[...] = o
    state_ref[...] = state * c_dec + _dot_tn((k * k_dec).astype(BF16), v)


def _ret_bwd_kernel(lg_ref, q_ref, k_ref, v_ref, cos_ref, sin_ref, oacc_ref, rg_ref, o_ref, state_ref):
    c = RET_CHUNK
    h = pl.program_id(1)
    lg_b = lg_ref[2 * h + 1]

    @pl.when(pl.program_id(2) == 0)
    def _():
        state_ref[...] = jnp.zeros_like(state_ref)

    cos = cos_ref[...]
    sin = sin_ref[...]
    q = _rotary(q_ref, cos, sin)
    k = _rotary(k_ref, cos, sin) * (RET_QK_DIM ** -0.5)
    v = v_ref[...]
    idx = lax.broadcasted_iota(jnp.int32, (c, 1), 0).astype(F32)
    q_dec = jnp.exp((c - idx) * lg_b)
    k_dec = jnp.exp(idx * lg_b)
    c_dec = jnp.exp(jnp.full((1, 1), c, F32) * lg_b)
    state = state_ref[...]
    o = oacc_ref[...] + _dot((q * q_dec).astype(BF16), state.astype(BF16))
    state_ref[...] = state * c_dec + _dot_tn((k * k_dec).astype(BF16), v)
    o = o * lax.rsqrt(jnp.mean(o * o, axis=-1, keepdims=True) + EPS)
    rg = rg_ref[...].astype(F32)
    o_ref[...] = (o * (rg * _sigmoid(rg))).astype(o_ref.dtype)


def _retention(proj, cos, sin, log_g, batch, seq):
    c = RET_CHUNK
    nc = seq // c
    t = batch * seq
    qk_blk = RET_QK_DIM
    v_blk = RET_V_DIM
    smem = pl.BlockSpec(memory_space=pltpu.SMEM)

    def row_f(b, h, n):
        return b * nc + n

    def row_b(b, h, n):
        return b * nc + (nc - 1 - n)

    def specs(row):
        return [
            pl.BlockSpec((c, qk_blk), lambda b, h, n: (row(b, h, n), OFF_RQ // qk_blk + h)),
            pl.BlockSpec((c, qk_blk), lambda b, h, n: (row(b, h, n), OFF_RK // qk_blk + h)),
            pl.BlockSpec((c, v_blk), lambda b, h, n: (row(b, h, n), OFF_RV // v_blk + h)),
        ]

    o_acc = pl.pallas_call(
        _ret_fwd_kernel,
        grid=(batch, RET_HEADS, nc),
        in_specs=[smem] + specs(row_f) + [
            pl.BlockSpec((c, qk_blk // 2), lambda b, h, n: (n, 0)),
            pl.BlockSpec((c, qk_blk // 2), lambda b, h, n: (n, 0)),
        ],
        out_specs=pl.BlockSpec((c, v_blk), lambda b, h, n: (row_f(b, h, n), h)),
        out_shape=jax.ShapeDtypeStruct((t, RET_V), F32),
        scratch_shapes=[pltpu.VMEM((RET_QK_DIM, RET_V_DIM), F32), pltpu.VMEM((c, c), F32)],
        compiler_params=_cparams(("parallel", "parallel", "arbitrary")),
        name="retention_fwd",
    )(log_g, proj, proj, proj, cos, sin)

    return pl.pallas_call(
        _ret_bwd_kernel,
        grid=(batch, RET_HEADS, nc),
        in_specs=[smem] + specs(row_b) + [
            pl.BlockSpec((c, qk_blk // 2), lambda b, h, n: (nc - 1 - n, 0)),
            pl.BlockSpec((c, qk_blk // 2), lambda b, h, n: (nc - 1 - n, 0)),
            pl.BlockSpec((c, v_blk), lambda b, h, n: (row_b(b, h, n), h)),
            pl.BlockSpec((c, v_blk), lambda b, h, n: (row_b(b, h, n), OFF_RG // v_blk + h)),
        ],
        out_specs=pl.BlockSpec((c, v_blk), lambda b, h, n: (row_b(b, h, n), h)),
        out_shape=jax.ShapeDtypeStruct((t, RET_V), BF16),
        scratch_shapes=[pltpu.VMEM((RET_QK_DIM, RET_V_DIM), F32)],
        compiler_params=_cparams(("parallel", "parallel", "arbitrary")),
        name="retention_bwd",
    )(log_g, proj, proj, proj, cos, sin, o_acc, proj)


DN_HALO = 16


def _dn_prep_kernel(x_ref, prev_ref, next_ref, w_ref, o_ref, *, tiles_per_seq):
    tr = x_ref.shape[0]
    i = pl.program_id(0)
    j = pl.program_id(1)
    pos = i % tiles_per_seq
    has_prev = (pos != 0).astype(F32)
    has_next = (pos != tiles_per_seq - 1).astype(F32)
    x = x_ref[...].astype(F32)
    xc = jnp.concatenate(
        [prev_ref[...].astype(F32) * has_prev, x, next_ref[...].astype(F32) * has_next], axis=0)
    n = tr + 2 * DN_HALO
    w = w_ref[...]
    lo, hi = DN_HALO, DN_HALO + tr
    y = (pltpu.roll(xc, 2, 0)[lo:hi] * w[0:1, :] + pltpu.roll(xc, 1, 0)[lo:hi] * w[1:2, :]
         + x * w[2:3, :] + pltpu.roll(xc, n - 1, 0)[lo:hi] * w[3:4, :])
    y = y * _sigmoid(y)
    tc = y.shape[1]
    qk_tiles = (2 * DN_QK) // tc
    q_tiles = DN_QK // tc
    scale = jnp.where(j < q_tiles, DN_DIM ** -0.5, 1.0).astype(F32)
    is_qk = j < qk_tiles
    outs = []
    for s in range(tc // DN_DIM):
        ys = y[:, s * DN_DIM:(s + 1) * DN_DIM]
        rs = lax.rsqrt(jnp.sum(ys * ys, axis=-1, keepdims=True) + EPS) * scale
        outs.append(ys * jnp.where(is_qk, rs, 1.0))
    o_ref[...] = jnp.concatenate(outs, axis=1).astype(o_ref.dtype)


def _dn_prep(proj, conv_w, seq, tr, tc):
    t = proj.shape[0]
    tiles_per_seq = seq // tr
    nrow = t // tr
    col0 = OFF_DQKV // tc
    hb = tr // DN_HALO
    last_halo = t // DN_HALO - 1
    return pl.pallas_call(
        functools.partial(_dn_prep_kernel, tiles_per_seq=tiles_per_seq),
        grid=(nrow, DN_CONV_CH // tc),
        in_specs=[
            pl.BlockSpec((tr, tc), lambda i, j: (i, col0 + j)),
            pl.BlockSpec((DN_HALO, tc), lambda i, j: (jnp.maximum(i * hb - 1, 0), col0 + j)),
            pl.BlockSpec((DN_HALO, tc), lambda i, j: (jnp.minimum((i + 1) * hb, last_halo), col0 + j)),
            pl.BlockSpec((DN_CONV, tc), lambda i, j: (0, j)),
        ],
        out_specs=pl.BlockSpec((tr, tc), lambda i, j: (i, j)),
        out_shape=jax.ShapeDtypeStruct((t, DN_CONV_CH), BF16),
        compiler_params=_cparams(("parallel", "parallel")),
        name="dn_prep",
    )(proj, proj, proj, conv_w)


def _blk(idx, size):
    return lax.shift_right_logical(idx, size.bit_length() - 1)


def _unit_tri_inverse(m, ci, cj, chunk):
    eye = (ci == cj).astype(F32)
    base = 8
    m8 = jnp.where(_blk(ci, base) == _blk(cj, base), m, 0.0)
    m8b = m8.astype(BF16)
    p2 = _dot(m8b, m8b)
    p2b = p2.astype(BF16)
    p4 = _dot(p2b, p2b)
    inv = _dot((eye - m8).astype(BF16), (eye + p2).astype(BF16))
    inv = _dot(inv.astype(BF16), (eye + p4).astype(BF16))
    s = base
    while s < chunk:
        off = jnp.where((_blk(ci, 2 * s) == _blk(cj, 2 * s)) & (_blk(ci, s) != _blk(cj, s)), m, 0.0)
        invb = inv.astype(BF16)
        inv = inv - _dot(_dot(invb, off.astype(BF16)).astype(BF16), invb)
        s *= 2
    return inv


def _dn_scan_kernel(qf_ref, kf_ref, vf_ref, gcf_ref, grf_ref,
                    qb_ref, kb_ref, vb_ref, gcb_ref, grb_ref,
                    of_ref, ob_ref, state_ref):
    r = DN_TILE
    c = DN_CHUNK
    nchunk = r // c
    grp = pl.program_id(1)

    @pl.when(pl.program_id(2) == 0)
    def _():
        state_ref[...] = jnp.zeros_like(state_ref)

    ci = lax.broadcasted_iota(jnp.int32, (r, r), 0)
    cj = lax.broadcasted_iota(jnp.int32, (r, r), 1)
    same = _blk(ci, c) == _blk(cj, c)
    lane = lax.broadcasted_iota(jnp.int32, (r, LANES), 1)
    same_f = same.astype(F32)

    for d, (q_ref, k_ref, v_ref, gc_ref, gr_ref, o_ref) in enumerate(
            ((qf_ref, kf_ref, vf_ref, gcf_ref, grf_ref, of_ref),
             (qb_ref, kb_ref, vb_ref, gcb_ref, grb_ref, ob_ref))):
        rev = d == 1
        before_eq = same & ((ci <= cj) if rev else (ci >= cj))
        strict = same & ((ci < cj) if rev else (ci > cj))
        q = q_ref[...]
        k = k_ref[...]
        kf32 = k.astype(F32)
        qf32 = q.astype(F32)
        kk = _dot_nt(k, k)
        qk = _dot_nt(q, k)
        gates = gc_ref[...]
        csum_all = jnp.dot(before_eq.astype(F32), gates, preferred_element_type=F32,
                           precision=lax.Precision.HIGHEST)
        ctot_all = jnp.dot(same_f, gates, preferred_element_type=F32, precision=lax.Precision.HIGHEST)
        grow = gr_ref[0, 0]
        at_or_after = same & ((ci >= cj) if rev else (ci <= cj))
        crow_all = jnp.dot(grow, at_or_after.astype(F32), preferred_element_type=F32,
                           precision=lax.Precision.HIGHEST)
        outs = []
        for hh in range(2):
            head = 2 * grp + hh
            beta_lane = d * DN_V_HEADS + head
            g_lane = (2 + d) * DN_V_HEADS + head
            beta_c = jnp.sum(jnp.where(lane == beta_lane, gates, 0.0), axis=-1, keepdims=True)
            gc_c = jnp.sum(jnp.where(lane == g_lane, csum_all, 0.0), axis=-1, keepdims=True)
            gt_c = jnp.sum(jnp.where(lane == g_lane, ctot_all, 0.0), axis=-1, keepdims=True)
            gc_r = crow_all[4 + 2 * d + hh:5 + 2 * d + hh, :]
            diff = gc_c - gc_r
            decay = jnp.where(before_eq, jnp.exp(jnp.where(before_eq, diff, 0.0)), 0.0)
            m = jnp.where(strict, beta_c * kk * decay, 0.0)
            tinv = _unit_tri_inverse(m, ci, cj, c)
            egc = jnp.exp(gc_c)
            v = v_ref[:, hh * DN_DIM:(hh + 1) * DN_DIM].astype(F32)
            rhs = jnp.concatenate([v * beta_c, kf32 * (beta_c * egc)], axis=1).astype(BF16)
            uw = _dot(tinv.astype(BF16), rhs)
            u = uw[:, :DN_DIM]
            w = uw[:, DN_DIM:]
            qe = (qf32 * egc).astype(BF16)
            ke = (kf32 * jnp.exp(gt_c - gc_c)).astype(BF16)
            qkm = (qk * decay).astype(BF16)
            wb = w.astype(BF16)
            sidx = 2 * d + hh
            state = state_ref[sidx]
            o_chunks = [None] * nchunk
            order = range(nchunk - 1, -1, -1) if rev else range(nchunk)
            for cc in order:
                lo, hi = cc * c, (cc + 1) * c
                sb = state.astype(BF16)
                ws = _dot(jnp.concatenate([wb[lo:hi], qe[lo:hi]], axis=0), sb)
                v_new = (u[lo:hi] - ws[:c]).astype(BF16)
                o_chunks[cc] = ws[c:] + _dot(qkm[lo:hi, lo:hi], v_new)
                state = state * jnp.exp(gt_c[lo:lo + 1, :]) + _dot_tn(ke[lo:hi], v_new)
            state_ref[sidx] = state
            outs.append(jnp.concatenate(o_chunks, axis=0))
        o_ref[...] = jnp.concatenate(outs, axis=1)


def _dn_scan(dn_qkv, gates, gates_rows, batch, seq):
    r = DN_TILE
    nt = seq // r
    t = batch * seq
    kcol0 = DN_QK // DN_DIM
    vcol0 = (2 * DN_QK) // (2 * DN_DIM)

    def row_f(b, g, n):
        return b * nt + n

    def row_b(b, g, n):
        return b * nt + (nt - 1 - n)

    def specs(row, tile):
        return [
            pl.BlockSpec((r, DN_DIM), lambda b, g, n: (row(b, g, n), g)),
            pl.BlockSpec((r, DN_DIM), lambda b, g, n: (row(b, g, n), kcol0 + g)),
            pl.BlockSpec((r, 2 * DN_DIM), lambda b, g, n: (row(b, g, n), vcol0 + g)),
            pl.BlockSpec((r, LANES), lambda b, g, n: (row(b, g, n), 0)),
            pl.BlockSpec((1, 1, 8, r), lambda b, g, n: (b, g, 0, tile(n))),
        ]

    out_shape = jax.ShapeDtypeStruct((t, DN_V), F32)
    return pl.pallas_call(
        _dn_scan_kernel,
        grid=(batch, DN_K_HEADS, nt),
        in_specs=specs(row_f, lambda n: n) + specs(row_b, lambda n: nt - 1 - n),
        out_specs=[
            pl.BlockSpec((r, 2 * DN_DIM), lambda b, g, n: (row_f(b, g, n), g)),
            pl.BlockSpec((r, 2 * DN_DIM), lambda b, g, n: (row_b(b, g, n), g)),
        ],
        out_shape=[out_shape, out_shape],
        scratch_shapes=[pltpu.VMEM((4, DN_DIM, DN_DIM), F32)],
        compiler_params=_cparams(("parallel", "parallel", "arbitrary")),
        name="dn_scan",
    )(dn_qkv, dn_qkv, dn_qkv, gates, gates_rows, dn_qkv, dn_qkv, dn_qkv, gates, gates_rows)


def _dn_final_kernel(of_ref, ob_ref, z_ref, nw_ref, o_ref):
    o = of_ref[...] + ob_ref[...]
    nw = nw_ref[...]
    outs = []
    for s in range(o.shape[1] // DN_DIM):
        os_ = o[:, s * DN_DIM:(s + 1) * DN_DIM]
        outs.append(os_ * lax.rsqrt(jnp.mean(os_ * os_, axis=-1, keepdims=True) + EPS) * nw)
    z = z_ref[...].astype(F32)
    o_ref[...] = (jnp.concatenate(outs, axis=1) * (z * _sigmoid(z))).astype(o_ref.dtype)


def _dn_final(o_f, o_b, proj, norm_w, tr, tc):
    t = o_f.shape[0]
    zcol0 = OFF_DZ // tc
    return pl.pallas_call(
        _dn_final_kernel,
        grid=(t // tr, DN_V // tc),
        in_specs=[
            pl.BlockSpec((tr, tc), lambda i, j: (i, j)),
            pl.BlockSpec((tr, tc), lambda i, j: (i, j)),
            pl.BlockSpec((tr, tc), lambda i, j: (i, zcol0 + j)),
            pl.BlockSpec((1, DN_DIM), lambda i, j: (0, 0)),
        ],
        out_specs=pl.BlockSpec((tr, tc), lambda i, j: (i, j)),
        out_shape=jax.ShapeDtypeStruct((t, DN_V), BF16),
        compiler_params=_cparams(("parallel", "parallel")),
        name="dn_final",
    )(o_f, o_b, proj, norm_w)


def _merge_kernel(ret_ref, dn_ref, wr_ref, wd_ref, gr_ref, gd_ref, o_ref):
    a = _dot(ret_ref[...], wr_ref[...])
    b = _dot(dn_ref[...], wd_ref[...])
    o_ref[...] = (_sigmoid(gr_ref[...].astype(F32)) * a + _sigmoid(gd_ref[...].astype(F32)) * b).astype(o_ref.dtype)


def _merge(ret, dn, w_br, w_bd, proj, tm, tn):
    t = ret.shape[0]
    return pl.pallas_call(
        _merge_kernel,
        grid=(t // tm, D_MODEL // tn),
        in_specs=[
            pl.BlockSpec((tm, RET_V), lambda i, j: (i, 0)),
            pl.BlockSpec((tm, DN_V), lambda i, j: (i, 0)),
            pl.BlockSpec((RET_V, tn), lambda i, j: (0, j)),
            pl.BlockSpec((DN_V, tn), lambda i, j: (0, j)),
            pl.BlockSpec((tm, tn), lambda i, j: (i, OFF_GR // tn + j)),
            pl.BlockSpec((tm, tn), lambda i, j: (i, OFF_GD // tn + j)),
        ],
        out_specs=pl.BlockSpec((tm, tn), lambda i, j: (i, j)),
        out_shape=jax.ShapeDtypeStruct((t, D_MODEL), BF16),
        compiler_params=_cparams(("parallel", "parallel")),
        name="branch_merge",
    )(ret, dn, w_br, w_bd, proj, proj)


def _out_proj_kernel(m_ref, w_ref, x_ref, o_ref):
    o_ref[...] = x_ref[...] + _dot(m_ref[...], w_ref[...])


def _out_proj(merged, w_out, x, tm, tn):
    t = merged.shape[0]
    return pl.pallas_call(
        _out_proj_kernel,
        grid=(t // tm, D_MODEL // tn),
        in_specs=[
            pl.BlockSpec((tm, D_MODEL), lambda i, j: (i, 0)),
            pl.BlockSpec((D_MODEL, tn), lambda i, j: (0, j)),
            pl.BlockSpec((tm, tn), lambda i, j: (i, j)),
        ],
        out_specs=pl.BlockSpec((tm, tn), lambda i, j: (i, j)),
        out_shape=jax.ShapeDtypeStruct((t, D_MODEL), F32),
        compiler_params=_cparams(("parallel", "parallel")),
        name="out_proj",
    )(merged, w_out, x)


def _router_kernel(h_ref, nw_ref, wr_ref, br_ref, xn_ref, route_ref, count_ref, carry_ref):
    tr = h_ref.shape[0]

    @pl.when(pl.program_id(0) == 0)
    def _():
        carry_ref[...] = jnp.zeros_like(carry_ref)

    h = h_ref[...]
    xn = h * lax.rsqrt(jnp.mean(h * h, axis=-1, keepdims=True) + EPS) * nw_ref[...]
    xn_ref[...] = xn
    logits = jnp.dot(xn, wr_ref[...], preferred_element_type=F32, precision=lax.Precision.HIGHEST) + br_ref[...]
    lane = lax.broadcasted_iota(jnp.int32, (tr, LANES), 1)
    neg = jnp.float32(-jnp.inf)
    work = jnp.where(lane < N_EXPERTS, logits, neg)
    vals, idxs = [], []
    onehot = jnp.zeros((tr, LANES), F32)
    for _ in range(TOP_K):
        mx = jnp.max(work, axis=-1, keepdims=True)
        ix = jnp.min(jnp.where(work == mx, lane, LANES), axis=-1, keepdims=True)
        sel = lane == ix
        onehot = jnp.where(sel, 1.0, onehot)
        work = jnp.where(sel, neg, work)
        vals.append(mx)
        idxs.append(ix)
    exps = [jnp.exp(v - vals[0]) for v in vals]
    denom = exps[0] + exps[1] + exps[2] + exps[3]
    ri = lax.broadcasted_iota(jnp.int32, (tr, tr), 0)
    rj = lax.broadcasted_iota(jnp.int32, (tr, tr), 1)
    lower = (ri > rj).astype(BF16)
    carry = carry_ref[0:1, :]
    rank = carry + _dot(lower, onehot.astype(BF16))
    out = jnp.zeros((tr, LANES), F32)
    for kk in range(TOP_K):
        rk = jnp.sum(jnp.where(lane == idxs[kk], rank, 0.0), axis=-1, keepdims=True)
        out = jnp.where(lane == kk, idxs[kk].astype(F32), out)
        out = jnp.where(lane == TOP_K + kk, exps[kk] / denom, out)
        out = jnp.where(lane == 2 * TOP_K + kk, rk, out)
    route_ref[...] = out
    new_carry = carry + jnp.sum(onehot, axis=0, keepdims=True)
    carry_ref[...] = jnp.broadcast_to(new_carry, carry_ref.shape)
    count_ref[...] = jnp.broadcast_to(new_carry, count_ref.shape)


def _router(h, nw, w_router_pad, b_router_pad, tr):
    t, d = h.shape
    return pl.pallas_call(
        _router_kernel,
        grid=(t // tr,),
        in_specs=[
            pl.BlockSpec((tr, d), lambda i: (i, 0)),
            pl.BlockSpec((1, d), lambda i: (0, 0)),
            pl.BlockSpec((d, LANES), lambda i: (0, 0)),
            pl.BlockSpec((1, LANES), lambda i: (0, 0)),
        ],
        out_specs=[
            pl.BlockSpec((tr, d), lambda i: (i, 0)),
            pl.BlockSpec((tr, LANES), lambda i: (i, 0)),
            pl.BlockSpec((8, LANES), lambda i: (0, 0)),
        ],
        out_shape=[
            jax.ShapeDtypeStruct((t, d), F32),
            jax.ShapeDtypeStruct((t, LANES), F32),
            jax.ShapeDtypeStruct((8, LANES), F32),
        ],
        scratch_shapes=[pltpu.VMEM((8, LANES), F32)],
        compiler_params=_cparams(("arbitrary",)),
        name="moe_router",
    )(h, nw, w_router_pad, b_router_pad)


def _row_copy(src_hbm, dst_ref, sem, src_row, dst_row):
    return pltpu.make_async_copy(src_hbm.at[pl.ds(src_row, 1), :], dst_ref.at[pl.ds(dst_row, 1), :], sem)


def _dispatch_kernel(idx_ref, src_hbm, o_ref, sem):
    tr = o_ref.shape[0]

    def start(r, carry):
        _row_copy(src_hbm, o_ref, sem, idx_ref[r], r).start()
        return carry

    lax.fori_loop(0, tr, start, 0)

    def wait(r, carry):
        _row_copy(src_hbm, o_ref, sem, 0, r).wait()
        return carry

    lax.fori_loop(0, tr, wait, 0)


def _dispatch(buf_tok, src, tr):
    n_rows = buf_tok.shape[0]
    d = src.shape[1]
    return pl.pallas_call(
        _dispatch_kernel,
        grid=(n_rows // tr,),
        in_specs=[
            pl.BlockSpec((tr,), lambda i: (i,), memory_space=pltpu.SMEM),
            pl.BlockSpec(memory_space=pl.ANY),
        ],
        out_specs=pl.BlockSpec((tr, d), lambda i: (i, 0)),
        out_shape=jax.ShapeDtypeStruct((n_rows, d), src.dtype),
        scratch_shapes=[pltpu.SemaphoreType.DMA(())],
        compiler_params=_cparams(("arbitrary",)),
        name="moe_dispatch",
    )(buf_tok, src)


def _expert_kernel(be_ref, nused_ref, x_ref, wg_ref, bg_ref, wu_ref, bu_ref, wd_ref, bd_ref, o_ref,
                   xb_ref, acc_ref):
    i = pl.program_id(0)
    j = pl.program_id(1)
    used = i < nused_ref[0]

    @pl.when(used & (j == 0))
    def _():
        xb_ref[...] = x_ref[...].astype(BF16)
        acc_ref[...] = jnp.zeros_like(acc_ref)

    @pl.when(used)
    def _():
        xb = xb_ref[...]
        gate = _dot(xb, wg_ref[...]) + bg_ref[...]
        up = _dot(xb, wu_ref[...]) + bu_ref[...]
        gate = jnp.minimum(gate, SWIGLU_LIMIT)
        up = jnp.clip(up, -SWIGLU_LIMIT, SWIGLU_LIMIT)
        hid = (up + 1.0) * gate * _sigmoid(SWIGLU_ALPHA * gate)
        acc_ref[...] += _dot(hid.astype(BF16), wd_ref[...])

    @pl.when(j == pl.num_programs(1) - 1)
    def _():
        o_ref[...] = jnp.where(used, acc_ref[...] + bd_ref[...], 0.0)


def _experts(block_e, n_used, x_sorted, w_gate, b_gate, w_up, b_up, w_down, b_down, tf):
    n_rows, d = x_sorted.shape
    nb = n_rows // MOE_BLOCK
    nf = D_FF // tf
    grid_spec = pltpu.PrefetchScalarGridSpec(
        num_scalar_prefetch=2,
        grid=(nb, nf),
        in_specs=[
            pl.BlockSpec((MOE_BLOCK, d), lambda i, j, be, nu: (i, 0)),
            pl.BlockSpec((None, d, tf), lambda i, j, be, nu: (be[i], 0, j)),
            pl.BlockSpec((None, 1, tf), lambda i, j, be, nu: (be[i], 0, j)),
            pl.BlockSpec((None, d, tf), lambda i, j, be, nu: (be[i], 0, j)),
            pl.BlockSpec((None, 1, tf), lambda i, j, be, nu: (be[i], 0, j)),
            pl.BlockSpec((None, tf, d), lambda i, j, be, nu: (be[i], j, 0)),
            pl.BlockSpec((None, 1, d), lambda i, j, be, nu: (be[i], 0, 0)),
        ],
        out_specs=pl.BlockSpec((MOE_BLOCK, d), lambda i, j, be, nu: (i, 0)),
        scratch_shapes=[pltpu.VMEM((MOE_BLOCK, d), BF16), pltpu.VMEM((MOE_BLOCK, d), F32)],
    )
    return pl.pallas_call(
        _expert_kernel,
        grid_spec=grid_spec,
        out_shape=jax.ShapeDtypeStruct((n_rows, d), F32),
        compiler_params=_cparams(("arbitrary", "arbitrary")),
        name="moe_experts",
    )(block_e, n_used, x_sorted, w_gate, b_gate, w_up, b_up, w_down, b_down)


def _combine_kernel(dest_ref, y_hbm, h_ref, route_ref, nw_ref, o_ref, buf_ref, sem):
    tr = h_ref.shape[0]

    def start(r, carry):
        for kk in range(TOP_K):
            _row_copy(y_hbm, buf_ref.at[kk], sem, dest_ref[r * TOP_K + kk], r).start()
        return carry

    lax.fori_loop(0, tr, start, 0)

    def wait(r, carry):
        for kk in range(TOP_K):
            _row_copy(y_hbm, buf_ref.at[kk], sem, 0, r).wait()
        return carry

    lax.fori_loop(0, tr, wait, 0)

    route = route_ref[...]
    acc = h_ref[...]
    for kk in range(TOP_K):
        acc = acc + route[:, TOP_K + kk:TOP_K + kk + 1] * buf_ref[kk]
    o_ref[...] = acc * lax.rsqrt(jnp.mean(acc * acc, axis=-1, keepdims=True) + EPS) * nw_ref[...]


def _combine(dest, y_sorted, h, route, nw, tr):
    t, d = h.shape
    return pl.pallas_call(
        _combine_kernel,
        grid=(t // tr,),
        in_specs=[
            pl.BlockSpec((tr * TOP_K,), lambda i: (i,), memory_space=pltpu.SMEM),
            pl.BlockSpec(memory_space=pl.ANY),
            pl.BlockSpec((tr, d), lambda i: (i, 0)),
            pl.BlockSpec((tr, LANES), lambda i: (i, 0)),
            pl.BlockSpec((1, d), lambda i: (0, 0)),
        ],
        out_specs=pl.BlockSpec((tr, d), lambda i: (i, 0)),
        out_shape=jax.ShapeDtypeStruct((t, d), F32),
        scratch_shapes=[pltpu.VMEM((TOP_K, tr, d), F32), pltpu.SemaphoreType.DMA(())],
        compiler_params=_cparams(("arbitrary",)),
        name="moe_combine",
    )(dest, y_sorted, h, route, nw)


def _deinterleave_perm():
    half = RET_QK_DIM // 2
    per_head = jnp.concatenate([jnp.arange(half) * 2, jnp.arange(half) * 2 + 1])
    return (jnp.arange(RET_HEADS)[:, None] * RET_QK_DIM + per_head[None, :]).reshape(-1)


def _split_w_in(w_in):
    o = 0
    parts = {}
    for name, width in (("rq", RET_QK), ("rk", RET_QK), ("rv", RET_V), ("rg", RET_V), ("dqkv", DN_CONV_CH),
                        ("dz", DN_V), ("small", 4 * DN_V_HEADS), ("gr", D_MODEL), ("gd", D_MODEL)):
        parts[name] = w_in[:, o:o + width]
        o += width
    perm = _deinterleave_perm()
    w_main = jnp.concatenate(
        [parts["rq"][:, perm], parts["rk"][:, perm], parts["rv"], parts["rg"], parts["dqkv"], parts["dz"],
         parts["gr"], parts["gd"]], axis=1).astype(BF16)
    w_small = jnp.pad(parts["small"], ((0, 0), (0, LANES - 4 * DN_V_HEADS))).astype(BF16)
    return w_main, w_small


def _token_mixer(x2, batch, seq, norm1_w, w_in, conv_w, a_log_f, a_log_b, dt_f, dt_b, dn_norm_w,
                 w_br, w_bd, w_out):
    t = batch * seq
    w_main, w_small = _split_w_in(w_in)
    nw1 = norm1_w.reshape(1, D_MODEL)
    proj = _norm_matmul(x2, nw1, w_main, BF16, tm=min(1024, t), tn=512)

    zeros = jnp.zeros((2 * DN_V_HEADS,), F32)
    pad = jnp.zeros((LANES - 4 * DN_V_HEADS,), F32)
    alog_row = jnp.concatenate([zeros, a_log_f, a_log_b, pad]).reshape(1, LANES)
    dtb_row = jnp.concatenate([zeros, dt_f, dt_b, pad]).reshape(1, LANES)
    gates = _gate_proj(x2, nw1, w_small, alog_row, dtb_row, tm=min(512, t))

    angle = 1.0 / (ROPE_BASE ** jnp.linspace(0.0, 1.0, RET_QK_DIM // 2, dtype=F32))
    theta = jnp.arange(seq, dtype=F32)[:, None] * angle[None, :]
    log_g = jnp.log(1.0 - 2.0 ** (-5.0 - jnp.arange(2 * RET_HEADS, dtype=F32)))
    ret = _retention(proj, jnp.cos(theta), jnp.sin(theta), log_g, batch, seq)

    dn_qkv = _dn_prep(proj, conv_w, seq, tr=min(512, seq), tc=512)
    gates_rows = gates[:, :4 * DN_V_HEADS].reshape(batch, seq, 4, DN_K_HEADS, 2)
    gates_rows = gates_rows.transpose(0, 3, 2, 4, 1).reshape(batch, DN_K_HEADS, 8, seq)
    o_f, o_b = _dn_scan(dn_qkv, gates, gates_rows, batch, seq)
    dn = _dn_final(o_f, o_b, proj, dn_norm_w.reshape(1, DN_DIM), tr=min(512, t), tc=512)

    merged = _merge(ret, dn, w_br.astype(BF16), w_bd.astype(BF16), proj, tm=min(512, t), tn=512)
    return _out_proj(merged, w_out.astype(BF16), x2, tm=min(512, t), tn=512)


def _moe(h, norm2_w, w_router, b_router, w_gate, b_gate, w_up, b_up, w_down, b_down, norm_f_w):
    t = h.shape[0]
    wr = jnp.pad(w_router, ((0, 0), (0, LANES - N_EXPERTS)))
    br = jnp.pad(b_router, (0, LANES - N_EXPERTS)).reshape(1, LANES)
    xn, route, counts = _router(h, norm2_w.reshape(1, D_MODEL), wr, br, tr=min(256, t))

    top_idx = route[:, :TOP_K].astype(jnp.int32)
    rank = route[:, 2 * TOP_K:3 * TOP_K].astype(jnp.int32)
    counts = counts[0, :N_EXPERTS].astype(jnp.int32)
    n_assign = t * TOP_K
    n_rows = ((n_assign + N_EXPERTS * (MOE_BLOCK - 1) + MOE_BLOCK - 1) // MOE_BLOCK) * MOE_BLOCK
    n_blocks = n_rows // MOE_BLOCK
    padded = ((counts + MOE_BLOCK - 1) // MOE_BLOCK) * MOE_BLOCK
    pad_ends = jnp.cumsum(padded)
    pad_starts = pad_ends - padded
    dest = (pad_starts[top_idx] + rank).reshape(-1)
    tok = jnp.repeat(jnp.arange(t, dtype=jnp.int32), TOP_K)
    buf_tok = jnp.zeros((n_rows,), jnp.int32).at[dest].set(tok)
    block_e = jnp.clip(jnp.searchsorted(pad_ends, jnp.arange(n_blocks, dtype=jnp.int32) * MOE_BLOCK,
                                        side="right"), 0, N_EXPERTS - 1).astype(jnp.int32)
    n_used = (pad_ends[-1] // MOE_BLOCK).astype(jnp.int32).reshape(1)

    x_sorted = _dispatch(buf_tok, xn, tr=MOE_BLOCK)
    y_sorted = _experts(block_e, n_used, x_sorted,
                        w_gate.astype(BF16), b_gate.reshape(N_EXPERTS, 1, D_FF),
                        w_up.astype(BF16), b_up.reshape(N_EXPERTS, 1, D_FF),
                        w_down.astype(BF16), b_down.reshape(N_EXPERTS, 1, D_MODEL), tf=512)
    return _combine(dest, y_sorted, h, route, norm_f_w.reshape(1, D_MODEL), tr=min(256, t))


def kernel(x, norm1_w, w_in, conv_w, dn_a_log_f, dn_a_log_b, dn_dt_bias_f, dn_dt_bias_b, dn_norm_w, w_branch_ret, w_branch_dn, w_out, norm2_w, w_router, b_router, w_gate, b_gate, w_up, b_up, w_down, b_down, norm_f_w):
    batch, seq, d = x.shape
    assert norm1_w.shape[0] == 1, "one layer"
    x2 = x.reshape(batch * seq, d)
    h = _token_mixer(x2, batch, seq, norm1_w[0], w_in[0], conv_w[0], dn_a_log_f[0], dn_a_log_b[0],
                     dn_dt_bias_f[0], dn_dt_bias_b[0], dn_norm_w[0], w_branch_ret[0], w_branch_dn[0], w_out[0])
    out = _moe(h, norm2_w[0], w_router[0], b_router[0], w_gate[0], b_gate[0], w_up[0], b_up[0],
               w_down[0], b_down[0], norm_f_w)
    return out.reshape(batch, seq, d)
```

```python
import functools

import jax
import jax.numpy as jnp
from jax import lax
from jax.experimental import pallas as pl
from jax.experimental.pallas import tpu as pltpu

F32 = jnp.float32
BF16 = jnp.bfloat16

D_MODEL = 2048
RET_HEADS = 4
RET_QK_DIM = 256
RET_V_DIM = 512
ROPE_BASE = 10000.0
DN_K_HEADS = 8
DN_V_HEADS = 16
DN_DIM = 128
DN_CONV = 4
DN_CHUNK = 64
N_EXPERTS = 32
TOP_K = 4
D_FF = 2048
SWIGLU_LIMIT = 7.0
SWIGLU_ALPHA = 1.702
EPS = 1e-6

RET_QK = RET_HEADS * RET_QK_DIM
RET_V = RET_HEADS * RET_V_DIM
DN_QK = DN_K_HEADS * DN_DIM
DN_V = DN_V_HEADS * DN_DIM
DN_CONV_CH = 2 * DN_QK + DN_V

OFF_RQ = 0
OFF_RK = OFF_RQ + RET_QK
OFF_RV = OFF_RK + RET_QK
OFF_RG = OFF_RV + RET_V
OFF_DQKV = OFF_RG + RET_V
OFF_DZ = OFF_DQKV + DN_CONV_CH
OFF_GR = OFF_DZ + DN_V
OFF_GD = OFF_GR + D_MODEL
N_MAIN = OFF_GD + D_MODEL
LANES = 128

RET_CHUNK = 256
DN_TILE = 256
MOE_BLOCK = 256
VMEM_LIMIT = 56 * 1024 * 1024


def _cparams(sem):
    return pltpu.CompilerParams(dimension_semantics=sem, vmem_limit_bytes=VMEM_LIMIT)


def _dot(a, b):
    return jnp.dot(a, b, preferred_element_type=F32)


def _dot_nt(a, b):
    return lax.dot_general(a, b, (((1,), (1,)), ((), ())), preferred_element_type=F32)


def _dot_tn(a, b):
    return lax.dot_general(a, b, (((0,), (0,)), ((), ())), preferred_element_type=F32)


def _sigmoid(x):
    return 1.0 / (1.0 + jnp.exp(-x))


def _norm_matmul_kernel(x_ref, nw_ref, w_ref, o_ref, xn_ref):
    @pl.when(pl.program_id(1) == 0)
    def _():
        x = x_ref[...]
        ms = jnp.mean(x * x, axis=-1, keepdims=True)
        xn_ref[...] = (x * lax.rsqrt(ms + EPS) * nw_ref[...]).astype(BF16)

    o_ref[...] = _dot(xn_ref[...], w_ref[...]).astype(o_ref.dtype)


def _norm_matmul(x, nw, w, out_dtype, tm, tn):
    t, d = x.shape
    n = w.shape[1]
    return pl.pallas_call(
        _norm_matmul_kernel,
        grid=(t // tm, n // tn),
        in_specs=[
            pl.BlockSpec((tm, d), lambda i, j: (i, 0)),
            pl.BlockSpec((1, d), lambda i, j: (0, 0)),
            pl.BlockSpec((d, tn), lambda i, j: (0, j)),
        ],
        out_specs=pl.BlockSpec((tm, tn), lambda i, j: (i, j)),
        out_shape=jax.ShapeDtypeStruct((t, n), out_dtype),
        scratch_shapes=[pltpu.VMEM((tm, d), BF16)],
        compiler_params=_cparams(("parallel", "arbitrary")),
        name="norm_in_proj",
    )(x, nw, w)


def _gate_proj_kernel(x_ref, nw_ref, w_ref, alog_ref, dtb_ref, o_ref):
    x = x_ref[...]
    ms = jnp.mean(x * x, axis=-1, keepdims=True)
    xn = (x * lax.rsqrt(ms + EPS) * nw_ref[...]).astype(BF16)
    p = _dot(xn, w_ref[...])
    lane = lax.broadcasted_iota(jnp.int32, p.shape, 1)
    beta = _sigmoid(p)
    z = p + dtb_ref[...]
    softplus = jnp.maximum(z, 0.0) + jnp.log(1.0 + jnp.exp(-jnp.abs(z)))
    g = -jnp.exp(alog_ref[...]) * softplus
    tm = p.shape[0]
    ci = lax.broadcasted_iota(jnp.int32, (tm, tm), 0)
    cj = lax.broadcasted_iota(jnp.int32, (tm, tm), 1)
    same = _blk(ci, DN_CHUNK) == _blk(cj, DN_CHUNK)

    def chunk_sum(mask):
        return jnp.dot(mask.astype(F32), g, preferred_element_type=F32, precision=lax.Precision.HIGHEST)

    csum_f = chunk_sum(same & (ci >= cj))
    csum_b = chunk_sum(same & (ci <= cj))
    ctot = pltpu.roll(chunk_sum(same), 2 * DN_V_HEADS, 1)
    nh = DN_V_HEADS
    o_ref[...] = jnp.where(lane < 2 * nh, beta,
                           jnp.where(lane < 3 * nh, csum_f,
                                     jnp.where(lane < 4 * nh, csum_b,
                                               jnp.where(lane < 6 * nh, ctot, 0.0))))


def _gate_proj(x, nw, w_small, alog_row, dtb_row, tm):
    t, d = x.shape
    return pl.pallas_call(
        _gate_proj_kernel,
        grid=(t // tm,),
        in_specs=[
            pl.BlockSpec((tm, d), lambda i: (i, 0)),
            pl.BlockSpec((1, d), lambda i: (0, 0)),
            pl.BlockSpec((d, LANES), lambda i: (0, 0)),
            pl.BlockSpec((1, LANES), lambda i: (0, 0)),
            pl.BlockSpec((1, LANES), lambda i: (0, 0)),
        ],
        out_specs=pl.BlockSpec((tm, LANES), lambda i: (i, 0)),
        out_shape=jax.ShapeDtypeStruct((t, LANES), F32),
        compiler_params=_cparams(("parallel",)),
        name="dn_gate_proj",
    )(x, nw, w_small, alog_row, dtb_row)


def _rotary(t_ref, cos, sin):
    t = t_ref[...].astype(F32)
    half = RET_QK_DIM // 2
    t1 = t[:, :half]
    t2 = t[:, half:]
    return jnp.concatenate([t1 * cos - t2 * sin, t2 * cos + t1 * sin], axis=1)


def _ret_fwd_kernel(lg_ref, q_ref, k_ref, v_ref, cos_ref, sin_ref, o_ref, state_ref, dmask_ref):
    c = RET_CHUNK
    h = pl.program_id(1)
    lg_f = lg_ref[2 * h]
    lg_b = lg_ref[2 * h + 1]

    @pl.when(pl.program_id(2) == 0)
    def _():
        state_ref[...] = jnp.zeros_like(state_ref)
        i = lax.broadcasted_iota(jnp.int32, (c, c), 0)
        j = lax.broadcasted_iota(jnp.int32, (c, c), 1)
        d = (i - j).astype(F32)
        dmask_ref[...] = jnp.where(d >= 0, jnp.exp(d * lg_f), jnp.exp(-d * lg_b))

    cos = cos_ref[...]
    sin = sin_ref[...]
    q = _rotary(q_ref, cos, sin)
    k = _rotary(k_ref, cos, sin) * (RET_QK_DIM ** -0.5)
    v = v_ref[...]
    idx = lax.broadcasted_iota(jnp.int32, (c, 1), 0).astype(F32)
    q_dec = jnp.exp((idx + 1.0) * lg_f)
    k_dec = jnp.exp((c - 1.0 - idx) * lg_f)
    c_dec = jnp.exp(jnp.full((1, 1), c, F32) * lg_f)
    qb = q.astype(BF16)
    kb = k.astype(BF16)
    scores = _dot_nt(qb, kb) * dmask_ref[...]
    state = state_ref[...]
    o = _dot(scores.astype(BF16), v) + _dot((q * q_dec).astype(BF16), state.astype(BF16))
    o_ref[...] = o
    state_ref[...] = state * c_dec + _dot_tn((k * k_dec).astype(BF16), v)


def _ret_bwd_kernel(lg_ref, q_ref, k_ref, v_ref, cos_ref, sin_ref, oacc_ref, rg_ref, o_ref, state_ref):
    c = RET_CHUNK
    h = pl.program_id(1)
    lg_b = lg_ref[2 * h + 1]

    @pl.when(pl.program_id(2) == 0)
    def _():
        state_ref[...] = jnp.zeros_like(state_ref)

    cos = cos_ref[...]
    sin = sin_ref[...]
    q = _rotary(q_ref, cos, sin)
    k = _rotary(k_ref, cos, sin) * (RET_QK_DIM ** -0.5)
    v = v_ref[...]
    idx = lax.broadcasted_iota(jnp.int32, (c, 1), 0).astype(F32)
    q_dec = jnp.exp((c - idx) * lg_b)
    k_dec = jnp.exp(idx * lg_b)
    c_dec = jnp.exp(jnp.full((1, 1), c, F32) * lg_b)
    state = state_ref[...]
    o = oacc_ref[...] + _dot((q * q_dec).astype(BF16), state.astype(BF16))
    state_ref[...] = state * c_dec + _dot_tn((k * k_dec).astype(BF16), v)
    o = o * lax.rsqrt(jnp.mean(o * o, axis=-1, keepdims=True) + EPS)
    rg = rg_ref[...].astype(F32)
    o_ref[...] = (o * (rg * _sigmoid(rg))).astype(o_ref.dtype)


def _retention(proj, cos, sin, log_g, batch, seq):
    c = RET_CHUNK
    nc = seq // c
    t = batch * seq
    qk_blk = RET_QK_DIM
    v_blk = RET_V_DIM
    smem = pl.BlockSpec(memory_space=pltpu.SMEM)

    def row_f(b, h, n):
        return b * nc + n

    def row_b(b, h, n):
        return b * nc + (nc - 1 - n)

    def specs(row):
        return [
            pl.BlockSpec((c, qk_blk), lambda b, h, n: (row(b, h, n), OFF_RQ // qk_blk + h)),
            pl.BlockSpec((c, qk_blk), lambda b, h, n: (row(b, h, n), OFF_RK // qk_blk + h)),
            pl.BlockSpec((c, v_blk), lambda b, h, n: (row(b, h, n), OFF_RV // v_blk + h)),
        ]

    o_acc = pl.pallas_call(
        _ret_fwd_kernel,
        grid=(batch, RET_HEADS, nc),
        in_specs=[smem] + specs(row_f) + [
            pl.BlockSpec((c, qk_blk // 2), lambda b, h, n: (n, 0)),
            pl.BlockSpec((c, qk_blk // 2), lambda b, h, n: (n, 0)),
        ],
        out_specs=pl.BlockSpec((c, v_blk), lambda b, h, n: (row_f(b, h, n), h)),
        out_shape=jax.ShapeDtypeStruct((t, RET_V), F32),
        scratch_shapes=[pltpu.VMEM((RET_QK_DIM, RET_V_DIM), F32), pltpu.VMEM((c, c), F32)],
        compiler_params=_cparams(("parallel", "parallel", "arbitrary")),
        name="retention_fwd",
    )(log_g, proj, proj, proj, cos, sin)

    return pl.pallas_call(
        _ret_bwd_kernel,
        grid=(batch, RET_HEADS, nc),
        in_specs=[smem] + specs(row_b) + [
            pl.BlockSpec((c, qk_blk // 2), lambda b, h, n: (nc - 1 - n, 0)),
            pl.BlockSpec((c, qk_blk // 2), lambda b, h, n: (nc - 1 - n, 0)),
            pl.BlockSpec((c, v_blk), lambda b, h, n: (row_b(b, h, n), h)),
            pl.BlockSpec((c, v_blk), lambda b, h, n: (row_b(b, h, n), OFF_RG // v_blk + h)),
        ],
        out_specs=pl.BlockSpec((c, v_blk), lambda b, h, n: (row_b(b, h, n), h)),
        out_shape=jax.ShapeDtypeStruct((t, RET_V), BF16),
        scratch_shapes=[pltpu.VMEM((RET_QK_DIM, RET_V_DIM), F32)],
        compiler_params=_cparams(("parallel", "parallel", "arbitrary")),
        name="retention_bwd",
    )(log_g, proj, proj, proj, cos, sin, o_acc, proj)


DN_HALO = 16


def _dn_prep_kernel(x_ref, prev_ref, next_ref, w_ref, o_ref, *, tiles_per_seq):
    tr = x_ref.shape[0]
    i = pl.program_id(0)
    j = pl.program_id(1)
    pos = i % tiles_per_seq
    has_prev = (pos != 0).astype(F32)
    has_next = (pos != tiles_per_seq - 1).astype(F32)
    x = x_ref[...].astype(F32)
    xc = jnp.concatenate(
        [prev_ref[...].astype(F32) * has_prev, x, next_ref[...].astype(F32) * has_next], axis=0)
    n = tr + 2 * DN_HALO
    w = w_ref[...]
    lo, hi = DN_HALO, DN_HALO + tr
    y = (pltpu.roll(xc, 2, 0)[lo:hi] * w[0:1, :] + pltpu.roll(xc, 1, 0)[lo:hi] * w[1:2, :]
         + x * w[2:3, :] + pltpu.roll(xc, n - 1, 0)[lo:hi] * w[3:4, :])
    y = y * _sigmoid(y)
    tc = y.shape[1]
    qk_tiles = (2 * DN_QK) // tc
    q_tiles = DN_QK // tc
    scale = jnp.where(j < q_tiles, DN_DIM ** -0.5, 1.0).astype(F32)
    is_qk = j < qk_tiles
    outs = []
    for s in range(tc // DN_DIM):
        ys = y[:, s * DN_DIM:(s + 1) * DN_DIM]
        rs = lax.rsqrt(jnp.sum(ys * ys, axis=-1, keepdims=True) + EPS) * scale
        outs.append(ys * jnp.where(is_qk, rs, 1.0))
    o_ref[...] = jnp.concatenate(outs, axis=1).astype(o_ref.dtype)


def _dn_prep(proj, conv_w, seq, tr, tc):
    t = proj.shape[0]
    tiles_per_seq = seq // tr
    nrow = t // tr
    col0 = OFF_DQKV // tc
    hb = tr // DN_HALO
    last_halo = t // DN_HALO - 1
    return pl.pallas_call(
        functools.partial(_dn_prep_kernel, tiles_per_seq=tiles_per_seq),
        grid=(nrow, DN_CONV_CH // tc),
        in_specs=[
            pl.BlockSpec((tr, tc), lambda i, j: (i, col0 + j)),
            pl.BlockSpec((DN_HALO, tc), lambda i, j: (jnp.maximum(i * hb - 1, 0), col0 + j)),
            pl.BlockSpec((DN_HALO, tc), lambda i, j: (jnp.minimum((i + 1) * hb, last_halo), col0 + j)),
            pl.BlockSpec((DN_CONV, tc), lambda i, j: (0, j)),
        ],
        out_specs=pl.BlockSpec((tr, tc), lambda i, j: (i, j)),
        out_shape=jax.ShapeDtypeStruct((t, DN_CONV_CH), BF16),
        compiler_params=_cparams(("parallel", "parallel")),
        name="dn_prep",
    )(proj, proj, proj, conv_w)


def _blk(idx, size):
    return lax.shift_right_logical(idx, size.bit_length() - 1)


def _unit_tri_inverse(ms, ci, cj, chunk):
    eye = (ci == cj).astype(F32)
    base = 8
    diag8 = _blk(ci, base) == _blk(cj, base)
    m8 = [jnp.where(diag8, m, 0.0) for m in ms]
    m8b = [m.astype(BF16) for m in m8]
    p2 = [_dot(m, m) for m in m8b]
    p2b = [p.astype(BF16) for p in p2]
    p4 = [_dot(p, p) for p in p2b]
    inv = [_dot((eye - m).astype(BF16), (eye + p).astype(BF16)) for m, p in zip(m8, p2)]
    inv = [_dot(t.astype(BF16), (eye + p).astype(BF16)) for t, p in zip(inv, p4)]
    s = base
    while s < chunk:
        offmask = (_blk(ci, 2 * s) == _blk(cj, 2 * s)) & (_blk(ci, s) != _blk(cj, s))
        invb = [t.astype(BF16) for t in inv]
        left = [_dot(t, jnp.where(offmask, m, 0.0).astype(BF16)).astype(BF16) for t, m in zip(invb, ms)]
        inv = [t - _dot(lf, tb) for t, lf, tb in zip(inv, left, invb)]
        s *= 2
    return inv


def _dn_scan_kernel(qf_ref, kf_ref, vf_ref, gcf_ref, grf_ref,
                    qb_ref, kb_ref, vb_ref, gcb_ref, grb_ref,
                    of_ref, ob_ref, state_ref):
    r = DN_TILE
    c = DN_CHUNK
    nchunk = r // c
    grp = pl.program_id(1)

    @pl.when(pl.program_id(2) == 0)
    def _():
        state_ref[...] = jnp.zeros_like(state_ref)

    ci = lax.broadcasted_iota(jnp.int32, (r, r), 0)
    cj = lax.broadcasted_iota(jnp.int32, (r, r), 1)
    same = _blk(ci, c) == _blk(cj, c)
    lane = lax.broadcasted_iota(jnp.int32, (r, LANES), 1)
    nh = DN_V_HEADS

    def column(gates, lane_idx):
        return jnp.sum(jnp.where(lane == lane_idx, gates, 0.0), axis=-1, keepdims=True)

    chains = []
    for d, (q_ref, k_ref, v_ref, gc_ref, gr_ref) in enumerate(
            ((qf_ref, kf_ref, vf_ref, gcf_ref, grf_ref), (qb_ref, kb_ref, vb_ref, gcb_ref, grb_ref))):
        rev = d == 1
        before_eq = same & ((ci <= cj) if rev else (ci >= cj))
        strict = same & ((ci < cj) if rev else (ci > cj))
        q = q_ref[...]
        k = k_ref[...]
        kf32 = k.astype(F32)
        qf32 = q.astype(F32)
        kk = _dot_nt(k, k)
        qk = _dot_nt(q, k)
        gates = gc_ref[...]
        grow = gr_ref[0, 0]
        for hh in range(2):
            head = 2 * grp + hh
            beta_c = column(gates, d * nh + head)
            gc_c = column(gates, (2 + d) * nh + head)
            gt_c = column(gates, (4 + d) * nh + head)
            gc_r = grow[2 * d + hh:2 * d + hh + 1, :]
            decay = jnp.where(before_eq, jnp.exp(jnp.where(before_eq, gc_c - gc_r, 0.0)), 0.0)
            egc = jnp.exp(gc_c)
            v = v_ref[:, hh * DN_DIM:(hh + 1) * DN_DIM].astype(F32)
            chains.append(dict(
                rev=rev, sidx=2 * d + hh,
                m=jnp.where(strict, beta_c * kk * decay, 0.0),
                rhs=jnp.concatenate([v * beta_c, kf32 * (beta_c * egc)], axis=1).astype(BF16),
                qe=(qf32 * egc).astype(BF16),
                ke=(kf32 * jnp.exp(gt_c - gc_c)).astype(BF16),
                qkm=(qk * decay).astype(BF16),
                egt=jnp.exp(gt_c)))
    tinvs = _unit_tri_inverse([ch["m"] for ch in chains], ci, cj, c)
    for ch, tinv in zip(chains, tinvs):
        uw = _dot(tinv.astype(BF16), ch["rhs"])
        ch["u"] = uw[:, :DN_DIM]
        ch["wb"] = uw[:, DN_DIM:].astype(BF16)
        ch["state"] = state_ref[ch["sidx"]]
        ch["o"] = [None] * nchunk
    for step in range(nchunk):
        for ch in chains:
            cc = nchunk - 1 - step if ch["rev"] else step
            lo, hi = cc * c, (cc + 1) * c
            sb = ch["state"].astype(BF16)
            ws = _dot(jnp.concatenate([ch["wb"][lo:hi], ch["qe"][lo:hi]], axis=0), sb)
            v_new = (ch["u"][lo:hi] - ws[:c]).astype(BF16)
            ch["o"][cc] = ws[c:] + _dot(ch["qkm"][lo:hi, lo:hi], v_new)
            ch["state"] = ch["state"] * ch["egt"][lo:lo + 1, :] + _dot_tn(ch["ke"][lo:hi], v_new)
    for ch in chains:
        state_ref[ch["sidx"]] = ch["state"]
    outs = [jnp.concatenate(ch["o"], axis=0) for ch in chains]
    of_ref[...] = jnp.concatenate(outs[:2], axis=1)
    ob_ref[...] = jnp.concatenate(outs[2:], axis=1)


def _dn_scan(dn_qkv, gates, gates_rows, batch, seq):
    r = DN_TILE
    nt = seq // r
    t = batch * seq
    kcol0 = DN_QK // DN_DIM
    vcol0 = (2 * DN_QK) // (2 * DN_DIM)

    def row_f(b, g, n):
        return b * nt + n

    def row_b(b, g, n):
        return b * nt + (nt - 1 - n)

    def specs(row, tile):
        return [
            pl.BlockSpec((r, DN_DIM), lambda b, g, n: (row(b, g, n), g)),
            pl.BlockSpec((r, DN_DIM), lambda b, g, n: (row(b, g, n), kcol0 + g)),
            pl.BlockSpec((r, 2 * DN_DIM), lambda b, g, n: (row(b, g, n), vcol0 + g)),
            pl.BlockSpec((r, LANES), lambda b, g, n: (row(b, g, n), 0)),
            pl.BlockSpec((1, 1, 8, r), lambda b, g, n: (b, g, 0, tile(n))),
        ]

    out_shape = jax.ShapeDtypeStruct((t, DN_V), F32)
    return pl.pallas_call(
        _dn_scan_kernel,
        grid=(batch, DN_K_HEADS, nt),
        in_specs=specs(row_f, lambda n: n) + specs(row_b, lambda n: nt - 1 - n),
        out_specs=[
            pl.BlockSpec((r, 2 * DN_DIM), lambda b, g, n: (row_f(b, g, n), g)),
            pl.BlockSpec((r, 2 * DN_DIM), lambda b, g, n: (row_b(b, g, n), g)),
        ],
        out_shape=[out_shape, out_shape],
        scratch_shapes=[pltpu.VMEM((4, DN_DIM, DN_DIM), F32)],
        compiler_params=_cparams(("parallel", "parallel", "arbitrary")),
        name="dn_scan",
    )(dn_qkv, dn_qkv, dn_qkv, gates, gates_rows, dn_qkv, dn_qkv, dn_qkv, gates, gates_rows)


def _dn_final_kernel(of_ref, ob_ref, z_ref, nw_ref, o_ref):
    o = of_ref[...] + ob_ref[...]
    nw = nw_ref[...]
    outs = []
    for s in range(o.shape[1] // DN_DIM):
        os_ = o[:, s * DN_DIM:(s + 1) * DN_DIM]
        outs.append(os_ * lax.rsqrt(jnp.mean(os_ * os_, axis=-1, keepdims=True) + EPS) * nw)
    z = z_ref[...].astype(F32)
    o_ref[...] = (jnp.concatenate(outs, axis=1) * (z * _sigmoid(z))).astype(o_ref.dtype)


def _dn_final(o_f, o_b, proj, norm_w, tr, tc):
    t = o_f.shape[0]
    zcol0 = OFF_DZ // tc
    return pl.pallas_call(
        _dn_final_kernel,
        grid=(t // tr, DN_V // tc),
        in_specs=[
            pl.BlockSpec((tr, tc), lambda i, j: (i, j)),
            pl.BlockSpec((tr, tc), lambda i, j: (i, j)),
            pl.BlockSpec((tr, tc), lambda i, j: (i, zcol0 + j)),
            pl.BlockSpec((1, DN_DIM), lambda i, j: (0, 0)),
        ],
        out_specs=pl.BlockSpec((tr, tc), lambda i, j: (i, j)),
        out_shape=jax.ShapeDtypeStruct((t, DN_V), BF16),
        compiler_params=_cparams(("parallel", "parallel")),
        name="dn_final",
    )(o_f, o_b, proj, norm_w)


def _merge_kernel(ret_ref, dn_ref, wr_ref, wd_ref, gr_ref, gd_ref, o_ref):
    a = _dot(ret_ref[...], wr_ref[...])
    b = _dot(dn_ref[...], wd_ref[...])
    o_ref[...] = (_sigmoid(gr_ref[...].astype(F32)) * a + _sigmoid(gd_ref[...].astype(F32)) * b).astype(o_ref.dtype)


def _merge(ret, dn, w_br, w_bd, proj, tm, tn):
    t = ret.shape[0]
    return pl.pallas_call(
        _merge_kernel,
        grid=(t // tm, D_MODEL // tn),
        in_specs=[
            pl.BlockSpec((tm, RET_V), lambda i, j: (i, 0)),
            pl.BlockSpec((tm, DN_V), lambda i, j: (i, 0)),
            pl.BlockSpec((RET_V, tn), lambda i, j: (0, j)),
            pl.BlockSpec((DN_V, tn), lambda i, j: (0, j)),
            pl.BlockSpec((tm, tn), lambda i, j: (i, OFF_GR // tn + j)),
            pl.BlockSpec((tm, tn), lambda i, j: (i, OFF_GD // tn + j)),
        ],
        out_specs=pl.BlockSpec((tm, tn), lambda i, j: (i, j)),
        out_shape=jax.ShapeDtypeStruct((t, D_MODEL), BF16),
        compiler_params=_cparams(("parallel", "parallel")),
        name="branch_merge",
    )(ret, dn, w_br, w_bd, proj, proj)


def _out_proj_kernel(m_ref, w_ref, x_ref, o_ref):
    o_ref[...] = x_ref[...] + _dot(m_ref[...], w_ref[...])


def _out_proj(merged, w_out, x, tm, tn):
    t = merged.shape[0]
    return pl.pallas_call(
        _out_proj_kernel,
        grid=(t // tm, D_MODEL // tn),
        in_specs=[
            pl.BlockSpec((tm, D_MODEL), lambda i, j: (i, 0)),
            pl.BlockSpec((D_MODEL, tn), lambda i, j: (0, j)),
            pl.BlockSpec((tm, tn), lambda i, j: (i, j)),
        ],
        out_specs=pl.BlockSpec((tm, tn), lambda i, j: (i, j)),
        out_shape=jax.ShapeDtypeStruct((t, D_MODEL), F32),
        compiler_params=_cparams(("parallel", "parallel")),
        name="out_proj",
    )(merged, w_out, x)


def _router_kernel(h_ref, nw_ref, wr_ref, br_ref, xn_ref, route_ref, count_ref, carry_ref):
    tr = h_ref.shape[0]

    @pl.when(pl.program_id(0) == 0)
    def _():
        carry_ref[...] = jnp.zeros_like(carry_ref)

    h = h_ref[...]
    xn = h * lax.rsqrt(jnp.mean(h * h, axis=-1, keepdims=True) + EPS) * nw_ref[...]
    xn_ref[...] = xn
    logits = jnp.dot(xn, wr_ref[...], preferred_element_type=F32, precision=lax.Precision.HIGHEST) + br_ref[...]
    lane = lax.broadcasted_iota(jnp.int32, (tr, LANES), 1)
    neg = jnp.float32(-jnp.inf)
    work = jnp.where(lane < N_EXPERTS, logits, neg)
    vals, idxs = [], []
    onehot = jnp.zeros((tr, LANES), F32)
    for _ in range(TOP_K):
        mx = jnp.max(work, axis=-1, keepdims=True)
        ix = jnp.min(jnp.where(work == mx, lane, LANES), axis=-1, keepdims=True)
        sel = lane == ix
        onehot = jnp.where(sel, 1.0, onehot)
        work = jnp.where(sel, neg, work)
        vals.append(mx)
        idxs.append(ix)
    exps = [jnp.exp(v - vals[0]) for v in vals]
    denom = exps[0] + exps[1] + exps[2] + exps[3]
    ri = lax.broadcasted_iota(jnp.int32, (tr, tr), 0)
    rj = lax.broadcasted_iota(jnp.int32, (tr, tr), 1)
    lower = (ri > rj).astype(BF16)
    carry = carry_ref[0:1, :]
    rank = carry + _dot(lower, onehot.astype(BF16))
    out = jnp.zeros((tr, LANES), F32)
    for kk in range(TOP_K):
        rk = jnp.sum(jnp.where(lane == idxs[kk], rank, 0.0), axis=-1, keepdims=True)
        out = jnp.where(lane == kk, idxs[kk].astype(F32), out)
        out = jnp.where(lane == TOP_K + kk, exps[kk] / denom, out)
        out = jnp.where(lane == 2 * TOP_K + kk, rk, out)
    route_ref[...] = out
    new_carry = carry + jnp.sum(onehot, axis=0, keepdims=True)
    carry_ref[...] = jnp.broadcast_to(new_carry, carry_ref.shape)
    count_ref[...] = jnp.broadcast_to(new_carry, count_ref.shape)


def _router(h, nw, w_router_pad, b_router_pad, tr):
    t, d = h.shape
    return pl.pallas_call(
        _router_kernel,
        grid=(t // tr,),
        in_specs=[
            pl.BlockSpec((tr, d), lambda i: (i, 0)),
            pl.BlockSpec((1, d), lambda i: (0, 0)),
            pl.BlockSpec((d, LANES), lambda i: (0, 0)),
            pl.BlockSpec((1, LANES), lambda i: (0, 0)),
        ],
        out_specs=[
            pl.BlockSpec((tr, d), lambda i: (i, 0)),
            pl.BlockSpec((tr, LANES), lambda i: (i, 0)),
            pl.BlockSpec((8, LANES), lambda i: (0, 0)),
        ],
        out_shape=[
            jax.ShapeDtypeStruct((t, d), F32),
            jax.ShapeDtypeStruct((t, LANES), F32),
            jax.ShapeDtypeStruct((8, LANES), F32),
        ],
        scratch_shapes=[pltpu.VMEM((8, LANES), F32)],
        compiler_params=_cparams(("arbitrary",)),
        name="moe_router",
    )(h, nw, w_router_pad, b_router_pad)


def _row_copy(src_hbm, dst_ref, sem, src_row, dst_row):
    return pltpu.make_async_copy(src_hbm.at[pl.ds(src_row, 1), :], dst_ref.at[pl.ds(dst_row, 1), :], sem)


def _dispatch_kernel(idx_ref, src_hbm, o_ref, sem):
    tr = o_ref.shape[0]

    def start(r, carry):
        _row_copy(src_hbm, o_ref, sem, idx_ref[r], r).start()
        return carry

    lax.fori_loop(0, tr, start, 0, unroll=8)

    def wait(r, carry):
        _row_copy(src_hbm, o_ref, sem, 0, r).wait()
        return carry

    lax.fori_loop(0, tr, wait, 0, unroll=8)


def _dispatch(buf_tok, src, tr):
    n_rows = buf_tok.shape[0]
    d = src.shape[1]
    return pl.pallas_call(
        _dispatch_kernel,
        grid=(n_rows // tr,),
        in_specs=[
            pl.BlockSpec((tr,), lambda i: (i,), memory_space=pltpu.SMEM),
            pl.BlockSpec(memory_space=pl.ANY),
        ],
        out_specs=pl.BlockSpec((tr, d), lambda i: (i, 0)),
        out_shape=jax.ShapeDtypeStruct((n_rows, d), src.dtype),
        scratch_shapes=[pltpu.SemaphoreType.DMA(())],
        compiler_params=_cparams(("arbitrary",)),
        name="moe_dispatch",
    )(buf_tok, src)


def _expert_kernel(be_ref, nused_ref, x_ref, wg_ref, bg_ref, wu_ref, bu_ref, wd_ref, bd_ref, o_ref,
                   xb_ref, acc_ref):
    i = pl.program_id(0)
    j = pl.program_id(1)
    used = i < nused_ref[0]

    @pl.when(used & (j == 0))
    def _():
        xb_ref[...] = x_ref[...].astype(BF16)
        acc_ref[...] = jnp.zeros_like(acc_ref)

    @pl.when(used)
    def _():
        xb = xb_ref[...]
        gate = _dot(xb, wg_ref[...]) + bg_ref[...]
        up = _dot(xb, wu_ref[...]) + bu_ref[...]
        gate = jnp.minimum(gate, SWIGLU_LIMIT)
        up = jnp.clip(up, -SWIGLU_LIMIT, SWIGLU_LIMIT)
        hid = (up + 1.0) * gate * _sigmoid(SWIGLU_ALPHA * gate)
        acc_ref[...] += _dot(hid.astype(BF16), wd_ref[...])

    @pl.when(j == pl.num_programs(1) - 1)
    def _():
        o_ref[...] = jnp.where(used, acc_ref[...] + bd_ref[...], 0.0)


def _experts(block_e, n_used, x_sorted, w_gate, b_gate, w_up, b_up, w_down, b_down, tf):
    n_rows, d = x_sorted.shape
    nb = n_rows // MOE_BLOCK
    nf = D_FF // tf
    grid_spec = pltpu.PrefetchScalarGridSpec(
        num_scalar_prefetch=2,
        grid=(nb, nf),
        in_specs=[
            pl.BlockSpec((MOE_BLOCK, d), lambda i, j, be, nu: (i, 0)),
            pl.BlockSpec((None, d, tf), lambda i, j, be, nu: (be[i], 0, j)),
            pl.BlockSpec((None, 1, tf), lambda i, j, be, nu: (be[i], 0, j)),
            pl.BlockSpec((None, d, tf), lambda i, j, be, nu: (be[i], 0, j)),
            pl.BlockSpec((None, 1, tf), lambda i, j, be, nu: (be[i], 0, j)),
            pl.BlockSpec((None, tf, d), lambda i, j, be, nu: (be[i], j, 0)),
            pl.BlockSpec((None, 1, d), lambda i, j, be, nu: (be[i], 0, 0)),
        ],
        out_specs=pl.BlockSpec((MOE_BLOCK, d), lambda i, j, be, nu: (i, 0)),
        scratch_shapes=[pltpu.VMEM((MOE_BLOCK, d), BF16), pltpu.VMEM((MOE_BLOCK, d), F32)],
    )
    return pl.pallas_call(
        _expert_kernel,
        grid_spec=grid_spec,
        out_shape=jax.ShapeDtypeStruct((n_rows, d), F32),
        compiler_params=_cparams(("arbitrary", "arbitrary")),
        name="moe_experts",
    )(block_e, n_used, x_sorted, w_gate, b_gate, w_up, b_up, w_down, b_down)


def _combine_kernel(dest_ref, y_hbm, h_ref, route_ref, nw_ref, o_ref, buf_ref, sem):
    tr = h_ref.shape[0]

    def start(r, carry):
        for kk in range(TOP_K):
            _row_copy(y_hbm, buf_ref.at[kk], sem, dest_ref[r * TOP_K + kk], r).start()
        return carry

    lax.fori_loop(0, tr, start, 0)

    def wait(r, carry):
        for kk in range(TOP_K):
            _row_copy(y_hbm, buf_ref.at[kk], sem, 0, r).wait()
        return carry

    lax.fori_loop(0, tr, wait, 0)

    route = route_ref[...]
    acc = h_ref[...]
    for kk in range(TOP_K):
        acc = acc + route[:, TOP_K + kk:TOP_K + kk + 1] * buf_ref[kk]
    o_ref[...] = acc * lax.rsqrt(jnp.mean(acc * acc, axis=-1, keepdims=True) + EPS) * nw_ref[...]


def _combine(dest, y_sorted, h, route, nw, tr):
    t, d = h.shape
    return pl.pallas_call(
        _combine_kernel,
        grid=(t // tr,),
        in_specs=[
            pl.BlockSpec((tr * TOP_K,), lambda i: (i,), memory_space=pltpu.SMEM),
            pl.BlockSpec(memory_space=pl.ANY),
            pl.BlockSpec((tr, d), lambda i: (i, 0)),
            pl.BlockSpec((tr, LANES), lambda i: (i, 0)),
            pl.BlockSpec((1, d), lambda i: (0, 0)),
        ],
        out_specs=pl.BlockSpec((tr, d), lambda i: (i, 0)),
        out_shape=jax.ShapeDtypeStruct((t, d), F32),
        scratch_shapes=[pltpu.VMEM((TOP_K, tr, d), F32), pltpu.SemaphoreType.DMA(())],
        compiler_params=_cparams(("arbitrary",)),
        name="moe_combine",
    )(dest, y_sorted, h, route, nw)


def _deinterleave_perm():
    half = RET_QK_DIM // 2
    per_head = jnp.concatenate([jnp.arange(half) * 2, jnp.arange(half) * 2 + 1])
    return (jnp.arange(RET_HEADS)[:, None] * RET_QK_DIM + per_head[None, :]).reshape(-1)


def _split_w_in(w_in):
    o = 0
    parts = {}
    for name, width in (("rq", RET_QK), ("rk", RET_QK), ("rv", RET_V), ("rg", RET_V), ("dqkv", DN_CONV_CH),
                        ("dz", DN_V), ("small", 4 * DN_V_HEADS), ("gr", D_MODEL), ("gd", D_MODEL)):
        parts[name] = w_in[:, o:o + width]
        o += width
    perm = _deinterleave_perm()
    w_main = jnp.concatenate(
        [parts["rq"][:, perm], parts["rk"][:, perm], parts["rv"], parts["rg"], parts["dqkv"], parts["dz"],
         parts["gr"], parts["gd"]], axis=1).astype(BF16)
    w_small = jnp.pad(parts["small"], ((0, 0), (0, LANES - 4 * DN_V_HEADS))).astype(BF16)
    return w_main, w_small


def _gate_rows(gates, batch, seq):
    nh = DN_V_HEADS
    rows = gates[:, 2 * nh:4 * nh].reshape(batch, seq, 2, DN_K_HEADS, 2)
    rows = rows.transpose(0, 3, 2, 4, 1).reshape(batch, DN_K_HEADS, 4, seq)
    return jnp.concatenate([rows, jnp.zeros_like(rows)], axis=2)


def _token_mixer(x2, batch, seq, norm1_w, w_in, conv_w, a_log_f, a_log_b, dt_f, dt_b, dn_norm_w,
                 w_br, w_bd, w_out):
    t = batch * seq
    w_main, w_small = _split_w_in(w_in)
    nw1 = norm1_w.reshape(1, D_MODEL)
    proj = _norm_matmul(x2, nw1, w_main, BF16, tm=min(1024, t), tn=512)

    zeros = jnp.zeros((2 * DN_V_HEADS,), F32)
    pad = jnp.zeros((LANES - 4 * DN_V_HEADS,), F32)
    alog_row = jnp.concatenate([zeros, a_log_f, a_log_b, pad]).reshape(1, LANES)
    dtb_row = jnp.concatenate([zeros, dt_f, dt_b, pad]).reshape(1, LANES)
    gates = _gate_proj(x2, nw1, w_small, alog_row, dtb_row, tm=min(512, t))

    angle = 1.0 / (ROPE_BASE ** jnp.linspace(0.0, 1.0, RET_QK_DIM // 2, dtype=F32))
    theta = jnp.arange(seq, dtype=F32)[:, None] * angle[None, :]
    log_g = jnp.log(1.0 - 2.0 ** (-5.0 - jnp.arange(2 * RET_HEADS, dtype=F32)))
    ret = _retention(proj, jnp.cos(theta), jnp.sin(theta), log_g, batch, seq)

    dn_qkv = _dn_prep(proj, conv_w, seq, tr=min(512, seq), tc=512)
    o_f, o_b = _dn_scan(dn_qkv, gates, _gate_rows(gates, batch, seq), batch, seq)
    dn = _dn_final(o_f, o_b, proj, dn_norm_w.reshape(1, DN_DIM), tr=min(512, t), tc=512)

    merged = _merge(ret, dn, w_br.astype(BF16), w_bd.astype(BF16), proj, tm=min(512, t), tn=512)
    return _out_proj(merged, w_out.astype(BF16), x2, tm=min(512, t), tn=512)


def _moe(h, norm2_w, w_router, b_router, w_gate, b_gate, w_up, b_up, w_down, b_down, norm_f_w):
    t = h.shape[0]
    wr = jnp.pad(w_router, ((0, 0), (0, LANES - N_EXPERTS)))
    br = jnp.pad(b_router, (0, LANES - N_EXPERTS)).reshape(1, LANES)
    xn, route, counts = _router(h, norm2_w.reshape(1, D_MODEL), wr, br, tr=min(256, t))

    top_idx = route[:, :TOP_K].astype(jnp.int32)
    rank = route[:, 2 * TOP_K:3 * TOP_K].astype(jnp.int32)
    counts = counts[0, :N_EXPERTS].astype(jnp.int32)
    n_assign = t * TOP_K
    n_rows = ((n_assign + N_EXPERTS * (MOE_BLOCK - 1) + MOE_BLOCK - 1) // MOE_BLOCK) * MOE_BLOCK
    n_blocks = n_rows // MOE_BLOCK
    padded = ((counts + MOE_BLOCK - 1) // MOE_BLOCK) * MOE_BLOCK
    pad_ends = jnp.cumsum(padded)
    pad_starts = pad_ends - padded
    dest = (pad_starts[top_idx] + rank).reshape(-1)
    tok = jnp.repeat(jnp.arange(t, dtype=jnp.int32), TOP_K)
    buf_tok = jnp.zeros((n_rows,), jnp.int32).at[dest].set(tok)
    block_e = jnp.clip(jnp.searchsorted(pad_ends, jnp.arange(n_blocks, dtype=jnp.int32) * MOE_BLOCK,
                                        side="right"), 0, N_EXPERTS - 1).astype(jnp.int32)
    n_used = (pad_ends[-1] // MOE_BLOCK).astype(jnp.int32).reshape(1)

    x_sorted = _dispatch(buf_tok, xn, tr=MOE_BLOCK)
    y_sorted = _experts(block_e, n_used, x_sorted,
                        w_gate.astype(BF16), b_gate.reshape(N_EXPERTS, 1, D_FF),
                        w_up.astype(BF16), b_up.reshape(N_EXPERTS, 1, D_FF),
                        w_down.astype(BF16), b_down.reshape(N_EXPERTS, 1, D_MODEL), tf=512)
    return _combine(dest, y_sorted, h, route, norm_f_w.reshape(1, D_MODEL), tr=min(256, t))


def kernel(x, norm1_w, w_in, conv_w, dn_a_log_f, dn_a_log_b, dn_dt_bias_f, dn_dt_bias_b, dn_norm_w, w_branch_ret, w_branch_dn, w_out, norm2_w, w_router, b_router, w_gate, b_gate, w_up, b_up, w_down, b_down, norm_f_w):
    batch, seq, d = x.shape
    assert norm1_w.shape[0] == 1, "one layer"
    x2 = x.reshape(batch * seq, d)
    h = _token_mixer(x2, batch, seq, norm1_w[0], w_in[0], conv_w[0], dn_a_log_f[0], dn_a_log_b[0],
                     dn_dt_bias_f[0], dn_dt_bias_b[0], dn_norm_w[0], w_branch_ret[0], w_branch_dn[0], w_out[0])
    out = _moe(h, norm2_w[0], w_router[0], b_router[0], w_gate[0], b_gate[0], w_up[0], b_up[0],
               w_down[0], b_down[0], norm_f_w)
    return out.reshape(batch, seq, d)
```

```python
import functools

import jax
import jax.numpy as jnp
from jax import lax
from jax.experimental import pallas as pl
from jax.experimental.pallas import tpu as pltpu

F32 = jnp.float32
BF16 = jnp.bfloat16

D_MODEL = 2048
RET_HEADS = 4
RET_QK_DIM = 256
RET_V_DIM = 512
ROPE_BASE = 10000.0
DN_K_HEADS = 8
DN_V_HEADS = 16
DN_DIM = 128
DN_CONV = 4
DN_CHUNK = 64
N_EXPERTS = 32
TOP_K = 4
D_FF = 2048
SWIGLU_LIMIT = 7.0
SWIGLU_ALPHA = 1.702
EPS = 1e-6

RET_QK = RET_HEADS * RET_QK_DIM
RET_V = RET_HEADS * RET_V_DIM
DN_QK = DN_K_HEADS * DN_DIM
DN_V = DN_V_HEADS * DN_DIM
DN_CONV_CH = 2 * DN_QK + DN_V

OFF_RQ = 0
OFF_RK = OFF_RQ + RET_QK
OFF_RV = OFF_RK + RET_QK
OFF_RG = OFF_RV + RET_V
OFF_DQKV = OFF_RG + RET_V
OFF_DZ = OFF_DQKV + DN_CONV_CH
OFF_GR = OFF_DZ + DN_V
OFF_GD = OFF_GR + D_MODEL
N_MAIN = OFF_GD + D_MODEL
LANES = 128

RET_CHUNK = 256
DN_TILE = 256
MOE_BLOCK = 256
MOE_GROUP = 8
VMEM_LIMIT = 56 * 1024 * 1024
EXPERT_VMEM_LIMIT = 60 * 1024 * 1024


def _cparams(sem):
    return pltpu.CompilerParams(dimension_semantics=sem, vmem_limit_bytes=VMEM_LIMIT)


def _dot(a, b):
    return jnp.dot(a, b, preferred_element_type=F32)


def _dot_nt(a, b):
    return lax.dot_general(a, b, (((1,), (1,)), ((), ())), preferred_element_type=F32)


def _dot_tn(a, b):
    return lax.dot_general(a, b, (((0,), (0,)), ((), ())), preferred_element_type=F32)


def _sigmoid(x):
    return 1.0 / (1.0 + jnp.exp(-x))


def _pack_bf16_pairs(x):
    n = x.shape[1] // 2
    bits = lax.bitcast_convert_type(x.astype(BF16).astype(F32), jnp.uint32)
    return (bits[:, n:] & jnp.uint32(0xFFFF0000)) | (bits[:, :n] >> 16)


def _unpack_bf16_pairs(p):
    lo = lax.bitcast_convert_type(p << 16, F32)
    hi = lax.bitcast_convert_type(p & jnp.uint32(0xFFFF0000), F32)
    return lo, hi


def _norm_matmul_kernel(x_ref, nw_ref, w_ref, o_ref, xn_ref):
    @pl.when(pl.program_id(1) == 0)
    def _():
        x = x_ref[...]
        ms = jnp.mean(x * x, axis=-1, keepdims=True)
        xn_ref[...] = (x * lax.rsqrt(ms + EPS) * nw_ref[...]).astype(BF16)

    o_ref[...] = _dot(xn_ref[...], w_ref[...]).astype(o_ref.dtype)


def _norm_matmul(x, nw, w, out_dtype, tm, tn):
    t, d = x.shape
    n = w.shape[1]
    return pl.pallas_call(
        _norm_matmul_kernel,
        grid=(t // tm, n // tn),
        in_specs=[
            pl.BlockSpec((tm, d), lambda i, j: (i, 0)),
            pl.BlockSpec((1, d), lambda i, j: (0, 0)),
            pl.BlockSpec((d, tn), lambda i, j: (0, j)),
        ],
        out_specs=pl.BlockSpec((tm, tn), lambda i, j: (i, j)),
        out_shape=jax.ShapeDtypeStruct((t, n), out_dtype),
        scratch_shapes=[pltpu.VMEM((tm, d), BF16)],
        compiler_params=_cparams(("parallel", "arbitrary")),
        name="norm_in_proj",
    )(x, nw, w)


def _gate_proj_kernel(x_ref, nw_ref, w_ref, alog_ref, dtb_ref, o_ref):
    x = x_ref[...]
    ms = jnp.mean(x * x, axis=-1, keepdims=True)
    xn = (x * lax.rsqrt(ms + EPS) * nw_ref[...]).astype(BF16)
    p = _dot(xn, w_ref[...])
    lane = lax.broadcasted_iota(jnp.int32, p.shape, 1)
    beta = _sigmoid(p)
    z = p + dtb_ref[...]
    softplus = jnp.maximum(z, 0.0) + jnp.log(1.0 + jnp.exp(-jnp.abs(z)))
    g = -jnp.exp(alog_ref[...]) * softplus
    tm = p.shape[0]
    ci = lax.broadcasted_iota(jnp.int32, (tm, tm), 0)
    cj = lax.broadcasted_iota(jnp.int32, (tm, tm), 1)
    same = _blk(ci, DN_CHUNK) == _blk(cj, DN_CHUNK)

    def chunk_sum(mask):
        return jnp.dot(mask.astype(F32), g, preferred_element_type=F32, precision=lax.Precision.HIGHEST)

    csum_f = chunk_sum(same & (ci >= cj))
    csum_b = chunk_sum(same & (ci <= cj))
    ctot = pltpu.roll(chunk_sum(same), 2 * DN_V_HEADS, 1)
    nh = DN_V_HEADS
    o_ref[...] = jnp.where(lane < 2 * nh, beta,
                           jnp.where(lane < 3 * nh, csum_f,
                                     jnp.where(lane < 4 * nh, csum_b,
                                               jnp.where(lane < 6 * nh, ctot, 0.0))))


def _gate_proj(x, nw, w_small, alog_row, dtb_row, tm):
    t, d = x.shape
    return pl.pallas_call(
        _gate_proj_kernel,
        grid=(t // tm,),
        in_specs=[
            pl.BlockSpec((tm, d), lambda i: (i, 0)),
            pl.BlockSpec((1, d), lambda i: (0, 0)),
            pl.BlockSpec((d, LANES), lambda i: (0, 0)),
            pl.BlockSpec((1, LANES), lambda i: (0, 0)),
            pl.BlockSpec((1, LANES), lambda i: (0, 0)),
        ],
        out_specs=pl.BlockSpec((tm, LANES), lambda i: (i, 0)),
        out_shape=jax.ShapeDtypeStruct((t, LANES), F32),
        compiler_params=_cparams(("parallel",)),
        name="dn_gate_proj",
    )(x, nw, w_small, alog_row, dtb_row)


def _rotary(t_ref, cos, sin):
    t = t_ref[...].astype(F32)
    half = RET_QK_DIM // 2
    t1 = t[:, :half]
    t2 = t[:, half:]
    return jnp.concatenate([t1 * cos - t2 * sin, t2 * cos + t1 * sin], axis=1)


def _ret_fwd_kernel(lg_ref, q_ref, k_ref, v_ref, cos_ref, sin_ref, o_ref, state_ref, dmask_ref):
    c = RET_CHUNK
    h = pl.program_id(1)
    lg_f = lg_ref[2 * h]
    lg_b = lg_ref[2 * h + 1]

    @pl.when(pl.program_id(2) == 0)
    def _():
        state_ref[...] = jnp.zeros_like(state_ref)
        i = lax.broadcasted_iota(jnp.int32, (c, c), 0)
        j = lax.broadcasted_iota(jnp.int32, (c, c), 1)
        d = (i - j).astype(F32)
        dmask_ref[...] = jnp.where(d >= 0, jnp.exp(d * lg_f), jnp.exp(-d * lg_b))

    cos = cos_ref[...]
    sin = sin_ref[...]
    q = _rotary(q_ref, cos, sin)
    k = _rotary(k_ref, cos, sin) * (RET_QK_DIM ** -0.5)
    v = v_ref[...]
    idx = lax.broadcasted_iota(jnp.int32, (c, 1), 0).astype(F32)
    q_dec = jnp.exp((idx + 1.0) * lg_f)
    k_dec = jnp.exp((c - 1.0 - idx) * lg_f)
    c_dec = jnp.exp(jnp.full((1, 1), c, F32) * lg_f)
    qb = q.astype(BF16)
    kb = k.astype(BF16)
    scores = _dot_nt(qb, kb) * dmask_ref[...]
    state = state_ref[...]
    o = _dot(scores.astype(BF16), v) + _dot((q * q_dec).astype(BF16), state.astype(BF16))
    o_ref[...] = o
    state_ref[...] = state * c_dec + _dot_tn((k * k_dec).astype(BF16), v)


def _ret_bwd_kernel(lg_ref, q_ref, k_ref, v_ref, cos_ref, sin_ref, oacc_ref, rg_ref, o_ref, state_ref):
    c = RET_CHUNK
    h = pl.program_id(1)
    lg_b = lg_ref[2 * h + 1]

    @pl.when(pl.program_id(2) == 0)
    def _():
        state_ref[...] = jnp.zeros_like(state_ref)

    cos = cos_ref[...]
    sin = sin_ref[...]
    q = _rotary(q_ref, cos, sin)
    k = _rotary(k_ref, cos, sin) * (RET_QK_DIM ** -0.5)
    v = v_ref[...]
    idx = lax.broadcasted_iota(jnp.int32, (c, 1), 0).astype(F32)
    q_dec = jnp.exp((c - idx) * lg_b)
    k_dec = jnp.exp(idx * lg_b)
    c_dec = jnp.exp(jnp.full((1, 1), c, F32) * lg_b)
    state = state_ref[...]
    o = oacc_ref[...] + _dot((q * q_dec).astype(BF16), state.astype(BF16))
    state_ref[...] = state * c_dec + _dot_tn((k * k_dec).astype(BF16), v)
    o = o * lax.rsqrt(jnp.mean(o * o, axis=-1, keepdims=True) + EPS)
    rg = rg_ref[...].astype(F32)
    o_ref[...] = (o * (rg * _sigmoid(rg))).astype(o_ref.dtype)


def _retention(proj, cos, sin, log_g, batch, seq):
    c = RET_CHUNK
    nc = seq // c
    t = batch * seq
    qk_blk = RET_QK_DIM
    v_blk = RET_V_DIM
    smem = pl.BlockSpec(memory_space=pltpu.SMEM)

    def row_f(b, h, n):
        return b * nc + n

    def row_b(b, h, n):
        return b * nc + (nc - 1 - n)

    def specs(row):
        return [
            pl.BlockSpec((c, qk_blk), lambda b, h, n: (row(b, h, n), OFF_RQ // qk_blk + h)),
            pl.BlockSpec((c, qk_blk), lambda b, h, n: (row(b, h, n), OFF_RK // qk_blk + h)),
            pl.BlockSpec((c, v_blk), lambda b, h, n: (row(b, h, n), OFF_RV // v_blk + h)),
        ]

    o_acc = pl.pallas_call(
        _ret_fwd_kernel,
        grid=(batch, RET_HEADS, nc),
        in_specs=[smem] + specs(row_f) + [
            pl.BlockSpec((c, qk_blk // 2), lambda b, h, n: (n, 0)),
            pl.BlockSpec((c, qk_blk // 2), lambda b, h, n: (n, 0)),
        ],
        out_specs=pl.BlockSpec((c, v_blk), lambda b, h, n: (row_f(b, h, n), h)),
        out_shape=jax.ShapeDtypeStruct((t, RET_V), F32),
        scratch_shapes=[pltpu.VMEM((RET_QK_DIM, RET_V_DIM), F32), pltpu.VMEM((c, c), F32)],
        compiler_params=_cparams(("parallel", "parallel", "arbitrary")),
        name="retention_fwd",
    )(log_g, proj, proj, proj, cos, sin)

    return pl.pallas_call(
        _ret_bwd_kernel,
        grid=(batch, RET_HEADS, nc),
        in_specs=[smem] + specs(row_b) + [
            pl.BlockSpec((c, qk_blk // 2), lambda b, h, n: (nc - 1 - n, 0)),
            pl.BlockSpec((c, qk_blk // 2), lambda b, h, n: (nc - 1 - n, 0)),
            pl.BlockSpec((c, v_blk), lambda b, h, n: (row_b(b, h, n), h)),
            pl.BlockSpec((c, v_blk), lambda b, h, n: (row_b(b, h, n), OFF_RG // v_blk + h)),
        ],
        out_specs=pl.BlockSpec((c, v_blk), lambda b, h, n: (row_b(b, h, n), h)),
        out_shape=jax.ShapeDtypeStruct((t, RET_V), BF16),
        scratch_shapes=[pltpu.VMEM((RET_QK_DIM, RET_V_DIM), F32)],
        compiler_params=_cparams(("parallel", "parallel", "arbitrary")),
        name="retention_bwd",
    )(log_g, proj, proj, proj, cos, sin, o_acc, proj)


DN_HALO = 16


def _dn_prep_kernel(x_ref, prev_ref, next_ref, w_ref, o_ref, *, tiles_per_seq):
    tr = x_ref.shape[0]
    i = pl.program_id(0)
    j = pl.program_id(1)
    pos = i % tiles_per_seq
    has_prev = (pos != 0).astype(F32)
    has_next = (pos != tiles_per_seq - 1).astype(F32)
    x = x_ref[...].astype(F32)
    xc = jnp.concatenate(
        [prev_ref[...].astype(F32) * has_prev, x, next_ref[...].astype(F32) * has_next], axis=0)
    n = tr + 2 * DN_HALO
    w = w_ref[...]
    lo, hi = DN_HALO, DN_HALO + tr
    y = (pltpu.roll(xc, 2, 0)[lo:hi] * w[0:1, :] + pltpu.roll(xc, 1, 0)[lo:hi] * w[1:2, :]
         + x * w[2:3, :] + pltpu.roll(xc, n - 1, 0)[lo:hi] * w[3:4, :])
    y = y * _sigmoid(y)
    tc = y.shape[1]
    qk_tiles = (2 * DN_QK) // tc
    q_tiles = DN_QK // tc
    scale = jnp.where(j < q_tiles, DN_DIM ** -0.5, 1.0).astype(F32)
    is_qk = j < qk_tiles
    outs = []
    for s in range(tc // DN_DIM):
        ys = y[:, s * DN_DIM:(s + 1) * DN_DIM]
        rs = lax.rsqrt(jnp.sum(ys * ys, axis=-1, keepdims=True) + EPS) * scale
        outs.append(ys * jnp.where(is_qk, rs, 1.0))
    o_ref[...] = jnp.concatenate(outs, axis=1).astype(o_ref.dtype)


def _dn_prep(proj, conv_w, seq, tr, tc):
    t = proj.shape[0]
    tiles_per_seq = seq // tr
    nrow = t // tr
    col0 = OFF_DQKV // tc
    hb = tr // DN_HALO
    last_halo = t // DN_HALO - 1
    return pl.pallas_call(
        functools.partial(_dn_prep_kernel, tiles_per_seq=tiles_per_seq),
        grid=(nrow, DN_CONV_CH // tc),
        in_specs=[
            pl.BlockSpec((tr, tc), lambda i, j: (i, col0 + j)),
            pl.BlockSpec((DN_HALO, tc), lambda i, j: (jnp.maximum(i * hb - 1, 0), col0 + j)),
            pl.BlockSpec((DN_HALO, tc), lambda i, j: (jnp.minimum((i + 1) * hb, last_halo), col0 + j)),
            pl.BlockSpec((DN_CONV, tc), lambda i, j: (0, j)),
        ],
        out_specs=pl.BlockSpec((tr, tc), lambda i, j: (i, j)),
        out_shape=jax.ShapeDtypeStruct((t, DN_CONV_CH), BF16),
        compiler_params=_cparams(("parallel", "parallel")),
        name="dn_prep",
    )(proj, proj, proj, conv_w)


def _blk(idx, size):
    return lax.shift_right_logical(idx, size.bit_length() - 1)


def _unit_tri_inverse(ms, ci, cj, chunk):
    eye = (ci == cj).astype(F32)
    base = 8
    diag8 = _blk(ci, base) == _blk(cj, base)
    m8 = [jnp.where(diag8, m, 0.0) for m in ms]
    m8b = [m.astype(BF16) for m in m8]
    p2 = [_dot(m, m) for m in m8b]
    p2b = [p.astype(BF16) for p in p2]
    p4 = [_dot(p, p) for p in p2b]
    inv = [_dot((eye - m).astype(BF16), (eye + p).astype(BF16)) for m, p in zip(m8, p2)]
    inv = [_dot(t.astype(BF16), (eye + p).astype(BF16)) for t, p in zip(inv, p4)]
    s = base
    while s < chunk:
        offmask = (_blk(ci, 2 * s) == _blk(cj, 2 * s)) & (_blk(ci, s) != _blk(cj, s))
        invb = [t.astype(BF16) for t in inv]
        left = [_dot(t, jnp.where(offmask, m, 0.0).astype(BF16)).astype(BF16) for t, m in zip(invb, ms)]
        inv = [t - _dot(lf, tb) for t, lf, tb in zip(inv, left, invb)]
        s *= 2
    return inv


def _dn_scan_kernel(qf_ref, kf_ref, vf_ref, gcf_ref, grf_ref,
                    qb_ref, kb_ref, vb_ref, gcb_ref, grb_ref,
                    of_ref, ob_ref, state_ref):
    r = DN_TILE
    c = DN_CHUNK
    nchunk = r // c
    grp = pl.program_id(1)

    @pl.when(pl.program_id(2) == 0)
    def _():
        state_ref[...] = jnp.zeros_like(state_ref)

    ci = lax.broadcasted_iota(jnp.int32, (r, r), 0)
    cj = lax.broadcasted_iota(jnp.int32, (r, r), 1)
    same = _blk(ci, c) == _blk(cj, c)
    lane = lax.broadcasted_iota(jnp.int32, (r, LANES), 1)
    nh = DN_V_HEADS

    def column(gates, lane_idx):
        return jnp.sum(jnp.where(lane == lane_idx, gates, 0.0), axis=-1, keepdims=True)

    chains = []
    for d, (q_ref, k_ref, v_ref, gc_ref, gr_ref) in enumerate(
            ((qf_ref, kf_ref, vf_ref, gcf_ref, grf_ref), (qb_ref, kb_ref, vb_ref, gcb_ref, grb_ref))):
        rev = d == 1
        before_eq = same & ((ci <= cj) if rev else (ci >= cj))
        strict = same & ((ci < cj) if rev else (ci > cj))
        q = q_ref[...]
        k = k_ref[...]
        kf32 = k.astype(F32)
        qf32 = q.astype(F32)
        kk = _dot_nt(k, k)
        qk = _dot_nt(q, k)
        gates = gc_ref[...]
        grow = gr_ref[0, 0]
        for hh in range(2):
            head = 2 * grp + hh
            beta_c = column(gates, d * nh + head)
            gc_c = column(gates, (2 + d) * nh + head)
            gt_c = column(gates, (4 + d) * nh + head)
            gc_r = grow[2 * d + hh:2 * d + hh + 1, :]
            decay = jnp.where(before_eq, jnp.exp(jnp.where(before_eq, gc_c - gc_r, 0.0)), 0.0)
            egc = jnp.exp(gc_c)
            v = v_ref[:, hh * DN_DIM:(hh + 1) * DN_DIM].astype(F32)
            chains.append(dict(
                rev=rev, sidx=2 * d + hh,
                m=jnp.where(strict, beta_c * kk * decay, 0.0),
                rhs=jnp.concatenate([v * beta_c, kf32 * (beta_c * egc)], axis=1).astype(BF16),
                qe=(qf32 * egc).astype(BF16),
                ke=(kf32 * jnp.exp(gt_c - gc_c)).astype(BF16),
                qkm=(qk * decay).astype(BF16),
                egt=jnp.exp(gt_c)))
    tinvs = _unit_tri_inverse([ch["m"] for ch in chains], ci, cj, c)
    for ch, tinv in zip(chains, tinvs):
        uw = _dot(tinv.astype(BF16), ch["rhs"])
        ch["u"] = uw[:, :DN_DIM]
        ch["wb"] = uw[:, DN_DIM:].astype(BF16)
        ch["state"] = state_ref[ch["sidx"]]
        ch["o"] = [None] * nchunk
    for step in range(nchunk):
        for ch in chains:
            cc = nchunk - 1 - step if ch["rev"] else step
            lo, hi = cc * c, (cc + 1) * c
            sb = ch["state"].astype(BF16)
            ws = _dot(jnp.concatenate([ch["wb"][lo:hi], ch["qe"][lo:hi]], axis=0), sb)
            v_new = (ch["u"][lo:hi] - ws[:c]).astype(BF16)
            ch["o"][cc] = ws[c:] + _dot(ch["qkm"][lo:hi, lo:hi], v_new)
            ch["state"] = ch["state"] * ch["egt"][lo:lo + 1, :] + _dot_tn(ch["ke"][lo:hi], v_new)
    for ch in chains:
        state_ref[ch["sidx"]] = ch["state"]
    outs = [jnp.concatenate(ch["o"], axis=0) for ch in chains]
    of_ref[...] = jnp.concatenate(outs[:2], axis=1)
    ob_ref[...] = jnp.concatenate(outs[2:], axis=1)


def _dn_scan(dn_qkv, gates, gates_rows, batch, seq):
    r = DN_TILE
    nt = seq // r
    t = batch * seq
    kcol0 = DN_QK // DN_DIM
    vcol0 = (2 * DN_QK) // (2 * DN_DIM)

    def row_f(b, g, n):
        return b * nt + n

    def row_b(b, g, n):
        return b * nt + (nt - 1 - n)

    def specs(row, tile):
        return [
            pl.BlockSpec((r, DN_DIM), lambda b, g, n: (row(b, g, n), g)),
            pl.BlockSpec((r, DN_DIM), lambda b, g, n: (row(b, g, n), kcol0 + g)),
            pl.BlockSpec((r, 2 * DN_DIM), lambda b, g, n: (row(b, g, n), vcol0 + g)),
            pl.BlockSpec((r, LANES), lambda b, g, n: (row(b, g, n), 0)),
            pl.BlockSpec((1, 1, 8, r), lambda b, g, n: (b, g, 0, tile(n))),
        ]

    out_shape = jax.ShapeDtypeStruct((t, DN_V), F32)
    return pl.pallas_call(
        _dn_scan_kernel,
        grid=(batch, DN_K_HEADS, nt),
        in_specs=specs(row_f, lambda n: n) + specs(row_b, lambda n: nt - 1 - n),
        out_specs=[
            pl.BlockSpec((r, 2 * DN_DIM), lambda b, g, n: (row_f(b, g, n), g)),
            pl.BlockSpec((r, 2 * DN_DIM), lambda b, g, n: (row_b(b, g, n), g)),
        ],
        out_shape=[out_shape, out_shape],
        scratch_shapes=[pltpu.VMEM((4, DN_DIM, DN_DIM), F32)],
        compiler_params=_cparams(("parallel", "parallel", "arbitrary")),
        name="dn_scan",
    )(dn_qkv, dn_qkv, dn_qkv, gates, gates_rows, dn_qkv, dn_qkv, dn_qkv, gates, gates_rows)


def _dn_final_kernel(of_ref, ob_ref, z_ref, nw_ref, o_ref):
    o = of_ref[...] + ob_ref[...]
    nw = nw_ref[...]
    outs = []
    for s in range(o.shape[1] // DN_DIM):
        os_ = o[:, s * DN_DIM:(s + 1) * DN_DIM]
        outs.append(os_ * lax.rsqrt(jnp.mean(os_ * os_, axis=-1, keepdims=True) + EPS) * nw)
    z = z_ref[...].astype(F32)
    o_ref[...] = (jnp.concatenate(outs, axis=1) * (z * _sigmoid(z))).astype(o_ref.dtype)


def _dn_final(o_f, o_b, proj, norm_w, tr, tc):
    t = o_f.shape[0]
    zcol0 = OFF_DZ // tc
    return pl.pallas_call(
        _dn_final_kernel,
        grid=(t // tr, DN_V // tc),
        in_specs=[
            pl.BlockSpec((tr, tc), lambda i, j: (i, j)),
            pl.BlockSpec((tr, tc), lambda i, j: (i, j)),
            pl.BlockSpec((tr, tc), lambda i, j: (i, zcol0 + j)),
            pl.BlockSpec((1, DN_DIM), lambda i, j: (0, 0)),
        ],
        out_specs=pl.BlockSpec((tr, tc), lambda i, j: (i, j)),
        out_shape=jax.ShapeDtypeStruct((t, DN_V), BF16),
        compiler_params=_cparams(("parallel", "parallel")),
        name="dn_final",
    )(o_f, o_b, proj, norm_w)


def _merge_kernel(ret_ref, dn_ref, wr_ref, wd_ref, gr_ref, gd_ref, o_ref):
    a = _dot(ret_ref[...], wr_ref[...])
    b = _dot(dn_ref[...], wd_ref[...])
    o_ref[...] = (_sigmoid(gr_ref[...].astype(F32)) * a + _sigmoid(gd_ref[...].astype(F32)) * b).astype(o_ref.dtype)


def _merge(ret, dn, w_br, w_bd, proj, tm, tn):
    t = ret.shape[0]
    return pl.pallas_call(
        _merge_kernel,
        grid=(t // tm, D_MODEL // tn),
        in_specs=[
            pl.BlockSpec((tm, RET_V), lambda i, j: (i, 0)),
            pl.BlockSpec((tm, DN_V), lambda i, j: (i, 0)),
            pl.BlockSpec((RET_V, tn), lambda i, j: (0, j)),
            pl.BlockSpec((DN_V, tn), lambda i, j: (0, j)),
            pl.BlockSpec((tm, tn), lambda i, j: (i, OFF_GR // tn + j)),
            pl.BlockSpec((tm, tn), lambda i, j: (i, OFF_GD // tn + j)),
        ],
        out_specs=pl.BlockSpec((tm, tn), lambda i, j: (i, j)),
        out_shape=jax.ShapeDtypeStruct((t, D_MODEL), BF16),
        compiler_params=_cparams(("parallel", "parallel")),
        name="branch_merge",
    )(ret, dn, w_br, w_bd, proj, proj)


def _out_proj_kernel(m_ref, w_ref, x_ref, o_ref):
    o_ref[...] = x_ref[...] + _dot(m_ref[...], w_ref[...])


def _out_proj(merged, w_out, x, tm, tn):
    t = merged.shape[0]
    return pl.pallas_call(
        _out_proj_kernel,
        grid=(t // tm, D_MODEL // tn),
        in_specs=[
            pl.BlockSpec((tm, D_MODEL), lambda i, j: (i, 0)),
            pl.BlockSpec((D_MODEL, tn), lambda i, j: (0, j)),
            pl.BlockSpec((tm, tn), lambda i, j: (i, j)),
        ],
        out_specs=pl.BlockSpec((tm, tn), lambda i, j: (i, j)),
        out_shape=jax.ShapeDtypeStruct((t, D_MODEL), F32),
        compiler_params=_cparams(("parallel", "parallel")),
        name="out_proj",
    )(merged, w_out, x)


def _router_kernel(h_ref, nw_ref, wr_ref, br_ref, xn_ref, route_ref, count_ref, carry_ref):
    tr = h_ref.shape[0]

    @pl.when(pl.program_id(0) == 0)
    def _():
        carry_ref[...] = jnp.zeros_like(carry_ref)

    h = h_ref[...]
    xn = h * lax.rsqrt(jnp.mean(h * h, axis=-1, keepdims=True) + EPS) * nw_ref[...]
    xn_ref[...] = _pack_bf16_pairs(xn)
    logits = jnp.dot(xn, wr_ref[...], preferred_element_type=F32, precision=lax.Precision.HIGHEST) + br_ref[...]
    lane = lax.broadcasted_iota(jnp.int32, (tr, LANES), 1)
    neg = jnp.float32(-jnp.inf)
    work = jnp.where(lane < N_EXPERTS, logits, neg)
    vals, idxs = [], []
    onehot = jnp.zeros((tr, LANES), F32)
    for _ in range(TOP_K):
        mx = jnp.max(work, axis=-1, keepdims=True)
        ix = jnp.min(jnp.where(work == mx, lane, LANES), axis=-1, keepdims=True)
        sel = lane == ix
        onehot = jnp.where(sel, 1.0, onehot)
        work = jnp.where(sel, neg, work)
        vals.append(mx)
        idxs.append(ix)
    exps = [jnp.exp(v - vals[0]) for v in vals]
    denom = exps[0] + exps[1] + exps[2] + exps[3]
    ri = lax.broadcasted_iota(jnp.int32, (tr, tr), 0)
    rj = lax.broadcasted_iota(jnp.int32, (tr, tr), 1)
    lower = (ri > rj).astype(BF16)
    carry = carry_ref[0:1, :]
    rank = carry + _dot(lower, onehot.astype(BF16))
    out = jnp.zeros((tr, LANES), F32)
    for kk in range(TOP_K):
        rk = jnp.sum(jnp.where(lane == idxs[kk], rank, 0.0), axis=-1, keepdims=True)
        out = jnp.where(lane == kk, idxs[kk].astype(F32), out)
        out = jnp.where(lane == TOP_K + kk, exps[kk] / denom, out)
        out = jnp.where(lane == 2 * TOP_K + kk, rk, out)
    route_ref[...] = out
    new_carry = carry + jnp.sum(onehot, axis=0, keepdims=True)
    carry_ref[...] = jnp.broadcast_to(new_carry, carry_ref.shape)
    count_ref[...] = jnp.broadcast_to(new_carry, count_ref.shape)


def _router(h, nw, w_router_pad, b_router_pad, tr):
    t, d = h.shape
    return pl.pallas_call(
        _router_kernel,
        grid=(t // tr,),
        in_specs=[
            pl.BlockSpec((tr, d), lambda i: (i, 0)),
            pl.BlockSpec((1, d), lambda i: (0, 0)),
            pl.BlockSpec((d, LANES), lambda i: (0, 0)),
            pl.BlockSpec((1, LANES), lambda i: (0, 0)),
        ],
        out_specs=[
            pl.BlockSpec((tr, d // 2), lambda i: (i, 0)),
            pl.BlockSpec((tr, LANES), lambda i: (i, 0)),
            pl.BlockSpec((8, LANES), lambda i: (0, 0)),
        ],
        out_shape=[
            jax.ShapeDtypeStruct((t, d // 2), jnp.uint32),
            jax.ShapeDtypeStruct((t, LANES), F32),
            jax.ShapeDtypeStruct((8, LANES), F32),
        ],
        scratch_shapes=[pltpu.VMEM((8, LANES), F32)],
        compiler_params=_cparams(("arbitrary",)),
        name="moe_router",
    )(h, nw, w_router_pad, b_router_pad)


def _row_copy(src_hbm, dst_ref, sem, src_row, dst_row):
    return pltpu.make_async_copy(src_hbm.at[pl.ds(src_row, 1), :], dst_ref.at[pl.ds(dst_row, 1), :], sem)


def _dispatch_kernel(idx_ref, src_hbm, o_ref, sem):
    tr = o_ref.shape[0]

    def start(r, carry):
        _row_copy(src_hbm, o_ref, sem, idx_ref[r], r).start()
        return carry

    lax.fori_loop(0, tr, start, 0, unroll=8)

    def wait(r, carry):
        _row_copy(src_hbm, o_ref, sem, 0, r).wait()
        return carry

    lax.fori_loop(0, tr, wait, 0, unroll=8)


def _dispatch(buf_tok, src, tr):
    n_rows = buf_tok.shape[0]
    d = src.shape[1]
    return pl.pallas_call(
        _dispatch_kernel,
        grid=(n_rows // tr,),
        in_specs=[
            pl.BlockSpec((tr,), lambda i: (i,), memory_space=pltpu.SMEM),
            pl.BlockSpec(memory_space=pl.ANY),
        ],
        out_specs=pl.BlockSpec((tr, d), lambda i: (i, 0)),
        out_shape=jax.ShapeDtypeStruct((n_rows, d), src.dtype),
        scratch_shapes=[pltpu.SemaphoreType.DMA(())],
        compiler_params=_cparams(("arbitrary",)),
        name="moe_dispatch",
    )(buf_tok, src)


def _expert_kernel(se_ref, ss_ref, sn_ref, x_ref, wg_ref, bg_ref, wu_ref, bu_ref, wd_ref, bd_ref, o_ref,
                   wgb_ref, wub_ref, wdb_ref, acc_ref):
    s = pl.program_id(0)
    j = pl.program_id(1)
    r = pl.program_id(2)
    nf = pl.num_programs(1)
    n = sn_ref[s]

    @pl.when((r == 0) & (n > 0))
    def _():
        wgb_ref[...] = wg_ref[...].astype(BF16)
        wub_ref[...] = wu_ref[...].astype(BF16)
        wdb_ref[...] = wd_ref[...].astype(BF16)

    @pl.when(r < n)
    def _():
        lo, hi = _unpack_bf16_pairs(x_ref[...])
        xb = jnp.concatenate([lo.astype(BF16), hi.astype(BF16)], axis=1)
        gate = _dot(xb, wgb_ref[...]) + bg_ref[...]
        up = _dot(xb, wub_ref[...]) + bu_ref[...]
        gate = jnp.minimum(gate, SWIGLU_LIMIT)
        up = jnp.clip(up, -SWIGLU_LIMIT, SWIGLU_LIMIT)
        hid = (up + 1.0) * gate * _sigmoid(SWIGLU_ALPHA * gate)
        part = _dot(hid.astype(BF16), wdb_ref[...])

        @pl.when(j == 0)
        def _():
            acc_ref[r] = part + bd_ref[...]

        @pl.when((j > 0) & (j < nf - 1))
        def _():
            acc_ref[r] += part

        @pl.when(j == nf - 1)
        def _():
            o_ref[...] = _pack_bf16_pairs(acc_ref[r] + part)


def _experts(sb_expert, sb_start, sb_count, x_sorted, w_gate, b_gate, w_up, b_up, w_down, b_down, tf):
    n_rows, dp = x_sorted.shape
    d = 2 * dp
    nf = D_FF // tf
    n_super = sb_expert.shape[0]

    def row_blk(s, r, ss, sn):
        return ss[s] + jnp.minimum(r, jnp.maximum(sn[s] - 1, 0))

    def tile(s, j, sn):
        return jnp.where(sn[s] > 0, j, nf - 1)

    grid_spec = pltpu.PrefetchScalarGridSpec(
        num_scalar_prefetch=3,
        grid=(n_super, nf, MOE_GROUP),
        in_specs=[
            pl.BlockSpec((MOE_BLOCK, dp), lambda s, j, r, se, ss, sn: (row_blk(s, r, ss, sn), 0)),
            pl.BlockSpec((None, d, tf), lambda s, j, r, se, ss, sn: (se[s], 0, tile(s, j, sn))),
            pl.BlockSpec((None, 1, tf), lambda s, j, r, se, ss, sn: (se[s], 0, tile(s, j, sn))),
            pl.BlockSpec((None, d, tf), lambda s, j, r, se, ss, sn: (se[s], 0, tile(s, j, sn))),
            pl.BlockSpec((None, 1, tf), lambda s, j, r, se, ss, sn: (se[s], 0, tile(s, j, sn))),
            pl.BlockSpec((None, tf, d), lambda s, j, r, se, ss, sn: (se[s], tile(s, j, sn), 0)),
            pl.BlockSpec((None, 1, d), lambda s, j, r, se, ss, sn: (se[s], 0, 0)),
        ],
        out_specs=pl.BlockSpec(
            (MOE_BLOCK, dp),
            lambda s, j, r, se, ss, sn: (jnp.where(j == nf - 1, row_blk(s, r, ss, sn), ss[s]), 0)),
        scratch_shapes=[pltpu.VMEM((d, tf), BF16), pltpu.VMEM((d, tf), BF16), pltpu.VMEM((tf, d), BF16),
                        pltpu.VMEM((MOE_GROUP, MOE_BLOCK, d), F32)],
    )
    return pl.pallas_call(
        _expert_kernel,
        grid_spec=grid_spec,
        out_shape=jax.ShapeDtypeStruct((n_rows, dp), jnp.uint32),
        input_output_aliases={3: 0},
        compiler_params=pltpu.CompilerParams(dimension_semantics=("arbitrary", "arbitrary", "arbitrary"),
                                             vmem_limit_bytes=EXPERT_VMEM_LIMIT),
        name="moe_experts",
    )(sb_expert, sb_start, sb_count, x_sorted, w_gate, b_gate, w_up, b_up, w_down, b_down)


def _combine_kernel(dest_ref, y_hbm, h_ref, route_ref, nw_ref, o_ref, buf_ref, sem):
    tr = h_ref.shape[0]

    def start(r, carry):
        for kk in range(TOP_K):
            _row_copy(y_hbm, buf_ref.at[kk], sem, dest_ref[r * TOP_K + kk], r).start()
        return carry

    lax.fori_loop(0, tr, start, 0, unroll=2)

    def wait(r, carry):
        for kk in range(TOP_K):
            _row_copy(y_hbm, buf_ref.at[kk], sem, 0, r).wait()
        return carry

    lax.fori_loop(0, tr, wait, 0, unroll=2)

    route = route_ref[...]
    h = h_ref[...]
    dp = h.shape[1] // 2
    acc_lo = h[:, :dp]
    acc_hi = h[:, dp:]
    for kk in range(TOP_K):
        w = route[:, TOP_K + kk:TOP_K + kk + 1]
        lo, hi = _unpack_bf16_pairs(buf_ref[kk])
        acc_lo = acc_lo + w * lo
        acc_hi = acc_hi + w * hi
    acc = jnp.concatenate([acc_lo, acc_hi], axis=1)
    o_ref[...] = acc * lax.rsqrt(jnp.mean(acc * acc, axis=-1, keepdims=True) + EPS) * nw_ref[...]


def _combine(dest, y_sorted, h, route, nw, tr):
    t, d = h.shape
    return pl.pallas_call(
        _combine_kernel,
        grid=(t // tr,),
        in_specs=[
            pl.BlockSpec((tr * TOP_K,), lambda i: (i,), memory_space=pltpu.SMEM),
            pl.BlockSpec(memory_space=pl.ANY),
            pl.BlockSpec((tr, d), lambda i: (i, 0)),
            pl.BlockSpec((tr, LANES), lambda i: (i, 0)),
            pl.BlockSpec((1, d), lambda i: (0, 0)),
        ],
        out_specs=pl.BlockSpec((tr, d), lambda i: (i, 0)),
        out_shape=jax.ShapeDtypeStruct((t, d), F32),
        scratch_shapes=[pltpu.VMEM((TOP_K, tr, d // 2), jnp.uint32), pltpu.SemaphoreType.DMA(())],
        compiler_params=_cparams(("arbitrary",)),
        name="moe_combine",
    )(dest, y_sorted, h, route, nw)


def _deinterleave_perm():
    half = RET_QK_DIM // 2
    per_head = jnp.concatenate([jnp.arange(half) * 2, jnp.arange(half) * 2 + 1])
    return (jnp.arange(RET_HEADS)[:, None] * RET_QK_DIM + per_head[None, :]).reshape(-1)


def _split_w_in(w_in):
    o = 0
    parts = {}
    for name, width in (("rq", RET_QK), ("rk", RET_QK), ("rv", RET_V), ("rg", RET_V), ("dqkv", DN_CONV_CH),
                        ("dz", DN_V), ("small", 4 * DN_V_HEADS), ("gr", D_MODEL), ("gd", D_MODEL)):
        parts[name] = w_in[:, o:o + width]
        o += width
    perm = _deinterleave_perm()
    w_main = jnp.concatenate(
        [parts["rq"][:, perm], parts["rk"][:, perm], parts["rv"], parts["rg"], parts["dqkv"], parts["dz"],
         parts["gr"], parts["gd"]], axis=1).astype(BF16)
    w_small = jnp.pad(parts["small"], ((0, 0), (0, LANES - 4 * DN_V_HEADS))).astype(BF16)
    return w_main, w_small


def _gate_rows(gates, batch, seq):
    nh = DN_V_HEADS
    rows = gates[:, 2 * nh:4 * nh].reshape(batch, seq, 2, DN_K_HEADS, 2)
    rows = rows.transpose(0, 3, 2, 4, 1).reshape(batch, DN_K_HEADS, 4, seq)
    return jnp.concatenate([rows, jnp.zeros_like(rows)], axis=2)


def _token_mixer(x2, batch, seq, norm1_w, w_in, conv_w, a_log_f, a_log_b, dt_f, dt_b, dn_norm_w,
                 w_br, w_bd, w_out):
    t = batch * seq
    w_main, w_small = _split_w_in(w_in)
    nw1 = norm1_w.reshape(1, D_MODEL)
    proj = _norm_matmul(x2, nw1, w_main, BF16, tm=min(1024, t), tn=512)

    zeros = jnp.zeros((2 * DN_V_HEADS,), F32)
    pad = jnp.zeros((LANES - 4 * DN_V_HEADS,), F32)
    alog_row = jnp.concatenate([zeros, a_log_f, a_log_b, pad]).reshape(1, LANES)
    dtb_row = jnp.concatenate([zeros, dt_f, dt_b, pad]).reshape(1, LANES)
    gates = _gate_proj(x2, nw1, w_small, alog_row, dtb_row, tm=min(512, t))

    angle = 1.0 / (ROPE_BASE ** jnp.linspace(0.0, 1.0, RET_QK_DIM // 2, dtype=F32))
    theta = jnp.arange(seq, dtype=F32)[:, None] * angle[None, :]
    log_g = jnp.log(1.0 - 2.0 ** (-5.0 - jnp.arange(2 * RET_HEADS, dtype=F32)))
    ret = _retention(proj, jnp.cos(theta), jnp.sin(theta), log_g, batch, seq)

    dn_qkv = _dn_prep(proj, conv_w, seq, tr=min(512, seq), tc=512)
    o_f, o_b = _dn_scan(dn_qkv, gates, _gate_rows(gates, batch, seq), batch, seq)
    dn = _dn_final(o_f, o_b, proj, dn_norm_w.reshape(1, DN_DIM), tr=min(512, t), tc=512)

    merged = _merge(ret, dn, w_br.astype(BF16), w_bd.astype(BF16), proj, tm=min(512, t), tn=512)
    return _out_proj(merged, w_out.astype(BF16), x2, tm=min(512, t), tn=512)


def _moe(h, norm2_w, w_router, b_router, w_gate, b_gate, w_up, b_up, w_down, b_down, norm_f_w):
    t = h.shape[0]
    wr = jnp.pad(w_router, ((0, 0), (0, LANES - N_EXPERTS)))
    br = jnp.pad(b_router, (0, LANES - N_EXPERTS)).reshape(1, LANES)
    xn, route, counts = _router(h, norm2_w.reshape(1, D_MODEL), wr, br, tr=min(256, t))

    top_idx = route[:, :TOP_K].astype(jnp.int32)
    rank = route[:, 2 * TOP_K:3 * TOP_K].astype(jnp.int32)
    counts = counts[0, :N_EXPERTS].astype(jnp.int32)
    n_assign = t * TOP_K
    n_rows = ((n_assign + N_EXPERTS * (MOE_BLOCK - 1) + MOE_BLOCK - 1) // MOE_BLOCK) * MOE_BLOCK
    n_blocks = n_rows // MOE_BLOCK
    padded = ((counts + MOE_BLOCK - 1) // MOE_BLOCK) * MOE_BLOCK
    pad_ends = jnp.cumsum(padded)
    pad_starts = pad_ends - padded
    dest = (pad_starts[top_idx] + rank).reshape(-1)
    tok = jnp.repeat(jnp.arange(t, dtype=jnp.int32), TOP_K)
    buf_tok = jnp.zeros((n_rows,), jnp.int32).at[dest].set(tok)
    nb_e = padded // MOE_BLOCK
    blk_start_e = pad_starts // MOE_BLOCK
    ns_e = (nb_e + MOE_GROUP - 1) // MOE_GROUP
    s_end_e = jnp.cumsum(ns_e)
    n_super = N_EXPERTS + n_blocks // MOE_GROUP
    sid = jnp.arange(n_super, dtype=jnp.int32)
    last = s_end_e[-1] - 1
    sid_c = jnp.minimum(sid, last)
    e_of = jnp.sum((sid_c[:, None] >= s_end_e[None, :]).astype(jnp.int32), axis=1)
    local = sid_c - (s_end_e - ns_e)[e_of]
    sb_start = blk_start_e[e_of] + local * MOE_GROUP
    sb_count = jnp.clip(nb_e[e_of] - local * MOE_GROUP, 0, MOE_GROUP)
    idle = sid > last
    sb_start = jnp.where(idle, sb_start + sb_count - 1, sb_start).astype(jnp.int32)
    sb_count = jnp.where(idle, 0, sb_count).astype(jnp.int32)

    x_sorted = _dispatch(buf_tok, xn, tr=4 * MOE_BLOCK)
    y_sorted = _experts(e_of.astype(jnp.int32), sb_start, sb_count, x_sorted,
                        w_gate, b_gate.reshape(N_EXPERTS, 1, D_FF),
                        w_up, b_up.reshape(N_EXPERTS, 1, D_FF),
                        w_down, b_down.reshape(N_EXPERTS, 1, D_MODEL), tf=512)
    return _combine(dest, y_sorted, h, route, norm_f_w.reshape(1, D_MODEL), tr=min(512, t))


def kernel(x, norm1_w, w_in, conv_w, dn_a_log_f, dn_a_log_b, dn_dt_bias_f, dn_dt_bias_b, dn_norm_w, w_branch_ret, w_branch_dn, w_out, norm2_w, w_router, b_router, w_gate, b_gate, w_up, b_up, w_down, b_down, norm_f_w):
    batch, seq, d = x.shape
    assert norm1_w.shape[0] == 1, "one layer"
    x2 = x.reshape(batch * seq, d)
    h = _token_mixer(x2, batch, seq, norm1_w[0], w_in[0], conv_w[0], dn_a_log_f[0], dn_a_log_b[0],
                     dn_dt_bias_f[0], dn_dt_bias_b[0], dn_norm_w[0], w_branch_ret[0], w_branch_dn[0], w_out[0])
    out = _moe(h, norm2_w[0], w_router[0], b_router[0], w_gate[0], b_gate[0], w_up[0], b_up[0],
               w_down[0], b_down[0], norm_f_w)
    return out.reshape(batch, seq, d)
```

```python
import functools

import jax
import jax.numpy as jnp
from jax import lax
from jax.experimental import pallas as pl
from jax.experimental.pallas import tpu as pltpu

F32 = jnp.float32
BF16 = jnp.bfloat16

D_MODEL = 2048
RET_HEADS = 4
RET_QK_DIM = 256
RET_V_DIM = 512
ROPE_BASE = 10000.0
DN_K_HEADS = 8
DN_V_HEADS = 16
DN_DIM = 128
DN_CONV = 4
DN_CHUNK = 64
N_EXPERTS = 32
TOP_K = 4
D_FF = 2048
SWIGLU_LIMIT = 7.0
SWIGLU_ALPHA = 1.702
EPS = 1e-6

RET_QK = RET_HEADS * RET_QK_DIM
RET_V = RET_HEADS * RET_V_DIM
DN_QK = DN_K_HEADS * DN_DIM
DN_V = DN_V_HEADS * DN_DIM
DN_CONV_CH = 2 * DN_QK + DN_V

OFF_RQ = 0
OFF_RK = OFF_RQ + RET_QK
OFF_RV = OFF_RK + RET_QK
OFF_RG = OFF_RV + RET_V
OFF_DQKV = OFF_RG + RET_V
OFF_DZ = OFF_DQKV + DN_CONV_CH
OFF_GR = OFF_DZ + DN_V
OFF_GD = OFF_GR + D_MODEL
N_MAIN = OFF_GD + D_MODEL
LANES = 128

RET_CHUNK = 256
DN_TILE = 256
MOE_BLOCK = 256
MOE_CHUNK = 4
VMEM_LIMIT = 56 * 1024 * 1024
EXPERT_VMEM_LIMIT = 60 * 1024 * 1024


def _cparams(sem):
    return pltpu.CompilerParams(dimension_semantics=sem, vmem_limit_bytes=VMEM_LIMIT)


def _dot(a, b):
    return jnp.dot(a, b, preferred_element_type=F32)


def _dot_nt(a, b):
    return lax.dot_general(a, b, (((1,), (1,)), ((), ())), preferred_element_type=F32)


def _dot_tn(a, b):
    return lax.dot_general(a, b, (((0,), (0,)), ((), ())), preferred_element_type=F32)


def _sigmoid(x):
    return 1.0 / (1.0 + jnp.exp(-x))


def _pack_bf16_pairs(x):
    n = x.shape[1] // 2
    bits = lax.bitcast_convert_type(x.astype(BF16).astype(F32), jnp.uint32)
    return (bits[:, n:] & jnp.uint32(0xFFFF0000)) | (bits[:, :n] >> 16)


def _unpack_bf16_pairs(p):
    lo = lax.bitcast_convert_type(p << 16, F32)
    hi = lax.bitcast_convert_type(p & jnp.uint32(0xFFFF0000), F32)
    return lo, hi


def _norm_matmul_kernel(x_ref, nw_ref, w_ref, o_ref, xn_ref):
    @pl.when(pl.program_id(1) == 0)
    def _():
        x = x_ref[...]
        ms = jnp.mean(x * x, axis=-1, keepdims=True)
        xn_ref[...] = (x * lax.rsqrt(ms + EPS) * nw_ref[...]).astype(BF16)

    o_ref[...] = _dot(xn_ref[...], w_ref[...]).astype(o_ref.dtype)


def _norm_matmul(x, nw, w, out_dtype, tm, tn):
    t, d = x.shape
    n = w.shape[1]
    return pl.pallas_call(
        _norm_matmul_kernel,
        grid=(t // tm, n // tn),
        in_specs=[
            pl.BlockSpec((tm, d), lambda i, j: (i, 0)),
            pl.BlockSpec((1, d), lambda i, j: (0, 0)),
            pl.BlockSpec((d, tn), lambda i, j: (0, j)),
        ],
        out_specs=pl.BlockSpec((tm, tn), lambda i, j: (i, j)),
        out_shape=jax.ShapeDtypeStruct((t, n), out_dtype),
        scratch_shapes=[pltpu.VMEM((tm, d), BF16)],
        compiler_params=_cparams(("parallel", "arbitrary")),
        name="norm_in_proj",
    )(x, nw, w)


def _gate_proj_kernel(x_ref, nw_ref, w_ref, alog_ref, dtb_ref, o_ref):
    x = x_ref[...]
    ms = jnp.mean(x * x, axis=-1, keepdims=True)
    xn = (x * lax.rsqrt(ms + EPS) * nw_ref[...]).astype(BF16)
    p = _dot(xn, w_ref[...])
    lane = lax.broadcasted_iota(jnp.int32, p.shape, 1)
    beta = _sigmoid(p)
    z = p + dtb_ref[...]
    softplus = jnp.maximum(z, 0.0) + jnp.log(1.0 + jnp.exp(-jnp.abs(z)))
    g = -jnp.exp(alog_ref[...]) * softplus
    tm = p.shape[0]
    ci = lax.broadcasted_iota(jnp.int32, (tm, tm), 0)
    cj = lax.broadcasted_iota(jnp.int32, (tm, tm), 1)
    same = _blk(ci, DN_CHUNK) == _blk(cj, DN_CHUNK)

    def chunk_sum(mask):
        return jnp.dot(mask.astype(F32), g, preferred_element_type=F32, precision=lax.Precision.HIGHEST)

    csum_f = chunk_sum(same & (ci >= cj))
    csum_b = chunk_sum(same & (ci <= cj))
    ctot = pltpu.roll(chunk_sum(same), 2 * DN_V_HEADS, 1)
    nh = DN_V_HEADS
    o_ref[...] = jnp.where(lane < 2 * nh, beta,
                           jnp.where(lane < 3 * nh, csum_f,
                                     jnp.where(lane < 4 * nh, csum_b,
                                               jnp.where(lane < 6 * nh, ctot, 0.0))))


def _gate_proj(x, nw, w_small, alog_row, dtb_row, tm):
    t, d = x.shape
    return pl.pallas_call(
        _gate_proj_kernel,
        grid=(t // tm,),
        in_specs=[
            pl.BlockSpec((tm, d), lambda i: (i, 0)),
            pl.BlockSpec((1, d), lambda i: (0, 0)),
            pl.BlockSpec((d, LANES), lambda i: (0, 0)),
            pl.BlockSpec((1, LANES), lambda i: (0, 0)),
            pl.BlockSpec((1, LANES), lambda i: (0, 0)),
        ],
        out_specs=pl.BlockSpec((tm, LANES), lambda i: (i, 0)),
        out_shape=jax.ShapeDtypeStruct((t, LANES), F32),
        compiler_params=_cparams(("parallel",)),
        name="dn_gate_proj",
    )(x, nw, w_small, alog_row, dtb_row)


def _rotary(t_ref, cos, sin):
    t = t_ref[...].astype(F32)
    half = RET_QK_DIM // 2
    t1 = t[:, :half]
    t2 = t[:, half:]
    return jnp.concatenate([t1 * cos - t2 * sin, t2 * cos + t1 * sin], axis=1)


def _ret_fwd_kernel(lg_ref, q_ref, k_ref, v_ref, cos_ref, sin_ref, o_ref, state_ref, dmask_ref):
    c = RET_CHUNK
    h = pl.program_id(1)
    lg_f = lg_ref[2 * h]
    lg_b = lg_ref[2 * h + 1]

    @pl.when(pl.program_id(2) == 0)
    def _():
        state_ref[...] = jnp.zeros_like(state_ref)
        i = lax.broadcasted_iota(jnp.int32, (c, c), 0)
        j = lax.broadcasted_iota(jnp.int32, (c, c), 1)
        d = (i - j).astype(F32)
        dmask_ref[...] = jnp.where(d >= 0, jnp.exp(d * lg_f), jnp.exp(-d * lg_b))

    cos = cos_ref[...]
    sin = sin_ref[...]
    q = _rotary(q_ref, cos, sin)
    k = _rotary(k_ref, cos, sin) * (RET_QK_DIM ** -0.5)
    v = v_ref[...]
    idx = lax.broadcasted_iota(jnp.int32, (c, 1), 0).astype(F32)
    q_dec = jnp.exp((idx + 1.0) * lg_f)
    k_dec = jnp.exp((c - 1.0 - idx) * lg_f)
    c_dec = jnp.exp(jnp.full((1, 1), c, F32) * lg_f)
    qb = q.astype(BF16)
    kb = k.astype(BF16)
    scores = _dot_nt(qb, kb) * dmask_ref[...]
    state = state_ref[...]
    o = _dot(scores.astype(BF16), v) + _dot((q * q_dec).astype(BF16), state.astype(BF16))
    o_ref[...] = o
    state_ref[...] = state * c_dec + _dot_tn((k * k_dec).astype(BF16), v)


def _ret_bwd_kernel(lg_ref, q_ref, k_ref, v_ref, cos_ref, sin_ref, oacc_ref, rg_ref, o_ref, state_ref):
    c = RET_CHUNK
    h = pl.program_id(1)
    lg_b = lg_ref[2 * h + 1]

    @pl.when(pl.program_id(2) == 0)
    def _():
        state_ref[...] = jnp.zeros_like(state_ref)

    cos = cos_ref[...]
    sin = sin_ref[...]
    q = _rotary(q_ref, cos, sin)
    k = _rotary(k_ref, cos, sin) * (RET_QK_DIM ** -0.5)
    v = v_ref[...]
    idx = lax.broadcasted_iota(jnp.int32, (c, 1), 0).astype(F32)
    q_dec = jnp.exp((c - idx) * lg_b)
    k_dec = jnp.exp(idx * lg_b)
    c_dec = jnp.exp(jnp.full((1, 1), c, F32) * lg_b)
    state = state_ref[...]
    o = oacc_ref[...] + _dot((q * q_dec).astype(BF16), state.astype(BF16))
    state_ref[...] = state * c_dec + _dot_tn((k * k_dec).astype(BF16), v)
    o = o * lax.rsqrt(jnp.mean(o * o, axis=-1, keepdims=True) + EPS)
    rg = rg_ref[...].astype(F32)
    o_ref[...] = (o * (rg * _sigmoid(rg))).astype(o_ref.dtype)


def _retention(proj, cos, sin, log_g, batch, seq):
    c = RET_CHUNK
    nc = seq // c
    t = batch * seq
    qk_blk = RET_QK_DIM
    v_blk = RET_V_DIM
    smem = pl.BlockSpec(memory_space=pltpu.SMEM)

    def row_f(b, h, n):
        return b * nc + n

    def row_b(b, h, n):
        return b * nc + (nc - 1 - n)

    def specs(row):
        return [
            pl.BlockSpec((c, qk_blk), lambda b, h, n: (row(b, h, n), OFF_RQ // qk_blk + h)),
            pl.BlockSpec((c, qk_blk), lambda b, h, n: (row(b, h, n), OFF_RK // qk_blk + h)),
            pl.BlockSpec((c, v_blk), lambda b, h, n: (row(b, h, n), OFF_RV // v_blk + h)),
        ]

    o_acc = pl.pallas_call(
        _ret_fwd_kernel,
        grid=(batch, RET_HEADS, nc),
        in_specs=[smem] + specs(row_f) + [
            pl.BlockSpec((c, qk_blk // 2), lambda b, h, n: (n, 0)),
            pl.BlockSpec((c, qk_blk // 2), lambda b, h, n: (n, 0)),
        ],
        out_specs=pl.BlockSpec((c, v_blk), lambda b, h, n: (row_f(b, h, n), h)),
        out_shape=jax.ShapeDtypeStruct((t, RET_V), F32),
        scratch_shapes=[pltpu.VMEM((RET_QK_DIM, RET_V_DIM), F32), pltpu.VMEM((c, c), F32)],
        compiler_params=_cparams(("parallel", "parallel", "arbitrary")),
        name="retention_fwd",
    )(log_g, proj, proj, proj, cos, sin)

    return pl.pallas_call(
        _ret_bwd_kernel,
        grid=(batch, RET_HEADS, nc),
        in_specs=[smem] + specs(row_b) + [
            pl.BlockSpec((c, qk_blk // 2), lambda b, h, n: (nc - 1 - n, 0)),
            pl.BlockSpec((c, qk_blk // 2), lambda b, h, n: (nc - 1 - n, 0)),
            pl.BlockSpec((c, v_blk), lambda b, h, n: (row_b(b, h, n), h)),
            pl.BlockSpec((c, v_blk), lambda b, h, n: (row_b(b, h, n), OFF_RG // v_blk + h)),
        ],
        out_specs=pl.BlockSpec((c, v_blk), lambda b, h, n: (row_b(b, h, n), h)),
        out_shape=jax.ShapeDtypeStruct((t, RET_V), BF16),
        scratch_shapes=[pltpu.VMEM((RET_QK_DIM, RET_V_DIM), F32)],
        compiler_params=_cparams(("parallel", "parallel", "arbitrary")),
        name="retention_bwd",
    )(log_g, proj, proj, proj, cos, sin, o_acc, proj)


DN_HALO = 16


def _dn_prep_kernel(x_ref, prev_ref, next_ref, w_ref, o_ref, *, tiles_per_seq):
    tr = x_ref.shape[0]
    i = pl.program_id(0)
    j = pl.program_id(1)
    pos = i % tiles_per_seq
    has_prev = (pos != 0).astype(F32)
    has_next = (pos != tiles_per_seq - 1).astype(F32)
    x = x_ref[...].astype(F32)
    xc = jnp.concatenate(
        [prev_ref[...].astype(F32) * has_prev, x, next_ref[...].astype(F32) * has_next], axis=0)
    n = tr + 2 * DN_HALO
    w = w_ref[...]
    lo, hi = DN_HALO, DN_HALO + tr
    y = (pltpu.roll(xc, 2, 0)[lo:hi] * w[0:1, :] + pltpu.roll(xc, 1, 0)[lo:hi] * w[1:2, :]
         + x * w[2:3, :] + pltpu.roll(xc, n - 1, 0)[lo:hi] * w[3:4, :])
    y = y * _sigmoid(y)
    tc = y.shape[1]
    qk_tiles = (2 * DN_QK) // tc
    q_tiles = DN_QK // tc
    scale = jnp.where(j < q_tiles, DN_DIM ** -0.5, 1.0).astype(F32)
    is_qk = j < qk_tiles
    outs = []
    for s in range(tc // DN_DIM):
        ys = y[:, s * DN_DIM:(s + 1) * DN_DIM]
        rs = lax.rsqrt(jnp.sum(ys * ys, axis=-1, keepdims=True) + EPS) * scale
        outs.append(ys * jnp.where(is_qk, rs, 1.0))
    o_ref[...] = jnp.concatenate(outs, axis=1).astype(o_ref.dtype)


def _dn_prep(proj, conv_w, seq, tr, tc):
    t = proj.shape[0]
    tiles_per_seq = seq // tr
    nrow = t // tr
    col0 = OFF_DQKV // tc
    hb = tr // DN_HALO
    last_halo = t // DN_HALO - 1
    return pl.pallas_call(
        functools.partial(_dn_prep_kernel, tiles_per_seq=tiles_per_seq),
        grid=(nrow, DN_CONV_CH // tc),
        in_specs=[
            pl.BlockSpec((tr, tc), lambda i, j: (i, col0 + j)),
            pl.BlockSpec((DN_HALO, tc), lambda i, j: (jnp.maximum(i * hb - 1, 0), col0 + j)),
            pl.BlockSpec((DN_HALO, tc), lambda i, j: (jnp.minimum((i + 1) * hb, last_halo), col0 + j)),
            pl.BlockSpec((DN_CONV, tc), lambda i, j: (0, j)),
        ],
        out_specs=pl.BlockSpec((tr, tc), lambda i, j: (i, j)),
        out_shape=jax.ShapeDtypeStruct((t, DN_CONV_CH), BF16),
        compiler_params=_cparams(("parallel", "parallel")),
        name="dn_prep",
    )(proj, proj, proj, conv_w)


def _blk(idx, size):
    return lax.shift_right_logical(idx, size.bit_length() - 1)


def _unit_tri_inverse(ms, ci, cj, chunk):
    eye = (ci == cj).astype(F32)
    base = 8
    diag8 = _blk(ci, base) == _blk(cj, base)
    m8 = [jnp.where(diag8, m, 0.0) for m in ms]
    m8b = [m.astype(BF16) for m in m8]
    p2 = [_dot(m, m) for m in m8b]
    p2b = [p.astype(BF16) for p in p2]
    p4 = [_dot(p, p) for p in p2b]
    inv = [_dot((eye - m).astype(BF16), (eye + p).astype(BF16)) for m, p in zip(m8, p2)]
    inv = [_dot(t.astype(BF16), (eye + p).astype(BF16)) for t, p in zip(inv, p4)]
    s = base
    while s < chunk:
        offmask = (_blk(ci, 2 * s) == _blk(cj, 2 * s)) & (_blk(ci, s) != _blk(cj, s))
        invb = [t.astype(BF16) for t in inv]
        left = [_dot(t, jnp.where(offmask, m, 0.0).astype(BF16)).astype(BF16) for t, m in zip(invb, ms)]
        inv = [t - _dot(lf, tb) for t, lf, tb in zip(inv, left, invb)]
        s *= 2
    return inv


def _dn_scan_kernel(qf_ref, kf_ref, vf_ref, gcf_ref, grf_ref,
                    qb_ref, kb_ref, vb_ref, gcb_ref, grb_ref,
                    of_ref, ob_ref, state_ref):
    r = DN_TILE
    c = DN_CHUNK
    nchunk = r // c
    grp = pl.program_id(1)

    @pl.when(pl.program_id(2) == 0)
    def _():
        state_ref[...] = jnp.zeros_like(state_ref)

    ci = lax.broadcasted_iota(jnp.int32, (r, r), 0)
    cj = lax.broadcasted_iota(jnp.int32, (r, r), 1)
    same = _blk(ci, c) == _blk(cj, c)
    lane = lax.broadcasted_iota(jnp.int32, (r, LANES), 1)
    nh = DN_V_HEADS

    def column(gates, lane_idx):
        return jnp.sum(jnp.where(lane == lane_idx, gates, 0.0), axis=-1, keepdims=True)

    chains = []
    for d, (q_ref, k_ref, v_ref, gc_ref, gr_ref) in enumerate(
            ((qf_ref, kf_ref, vf_ref, gcf_ref, grf_ref), (qb_ref, kb_ref, vb_ref, gcb_ref, grb_ref))):
        rev = d == 1
        before_eq = same & ((ci <= cj) if rev else (ci >= cj))
        strict = same & ((ci < cj) if rev else (ci > cj))
        q = q_ref[...]
        k = k_ref[...]
        kf32 = k.astype(F32)
        qf32 = q.astype(F32)
        kk = _dot_nt(k, k)
        qk = _dot_nt(q, k)
        gates = gc_ref[...]
        grow = gr_ref[0, 0]
        for hh in range(2):
            head = 2 * grp + hh
            beta_c = column(gates, d * nh + head)
            gc_c = column(gates, (2 + d) * nh + head)
            gt_c = column(gates, (4 + d) * nh + head)
            gc_r = grow[2 * d + hh:2 * d + hh + 1, :]
            decay = jnp.where(before_eq, jnp.exp(jnp.where(before_eq, gc_c - gc_r, 0.0)), 0.0)
            egc = jnp.exp(gc_c)
            v = v_ref[:, hh * DN_DIM:(hh + 1) * DN_DIM].astype(F32)
            chains.append(dict(
                rev=rev, sidx=2 * d + hh,
                m=jnp.where(strict, beta_c * kk * decay, 0.0),
                rhs=jnp.concatenate([v * beta_c, kf32 * (beta_c * egc)], axis=1).astype(BF16),
                qe=(qf32 * egc).astype(BF16),
                ke=(kf32 * jnp.exp(gt_c - gc_c)).astype(BF16),
                qkm=(qk * decay).astype(BF16),
                egt=jnp.exp(gt_c)))
    tinvs = _unit_tri_inverse([ch["m"] for ch in chains], ci, cj, c)
    for ch, tinv in zip(chains, tinvs):
        uw = _dot(tinv.astype(BF16), ch["rhs"])
        ch["u"] = uw[:, :DN_DIM]
        ch["wb"] = uw[:, DN_DIM:].astype(BF16)
        ch["state"] = state_ref[ch["sidx"]]
        ch["o"] = [None] * nchunk
    for step in range(nchunk):
        for ch in chains:
            cc = nchunk - 1 - step if ch["rev"] else step
            lo, hi = cc * c, (cc + 1) * c
            sb = ch["state"].astype(BF16)
            ws = _dot(jnp.concatenate([ch["wb"][lo:hi], ch["qe"][lo:hi]], axis=0), sb)
            v_new = (ch["u"][lo:hi] - ws[:c]).astype(BF16)
            ch["o"][cc] = ws[c:] + _dot(ch["qkm"][lo:hi, lo:hi], v_new)
            ch["state"] = ch["state"] * ch["egt"][lo:lo + 1, :] + _dot_tn(ch["ke"][lo:hi], v_new)
    for ch in chains:
        state_ref[ch["sidx"]] = ch["state"]
    outs = [jnp.concatenate(ch["o"], axis=0) for ch in chains]
    of_ref[...] = jnp.concatenate(outs[:2], axis=1)
    ob_ref[...] = jnp.concatenate(outs[2:], axis=1)


def _dn_scan(dn_qkv, gates, gates_rows, batch, seq):
    r = DN_TILE
    nt = seq // r
    t = batch * seq
    kcol0 = DN_QK // DN_DIM
    vcol0 = (2 * DN_QK) // (2 * DN_DIM)

    def row_f(b, g, n):
        return b * nt + n

    def row_b(b, g, n):
        return b * nt + (nt - 1 - n)

    def specs(row, tile):
        return [
            pl.BlockSpec((r, DN_DIM), lambda b, g, n: (row(b, g, n), g)),
            pl.BlockSpec((r, DN_DIM), lambda b, g, n: (row(b, g, n), kcol0 + g)),
            pl.BlockSpec((r, 2 * DN_DIM), lambda b, g, n: (row(b, g, n), vcol0 + g)),
            pl.BlockSpec((r, LANES), lambda b, g, n: (row(b, g, n), 0)),
            pl.BlockSpec((1, 1, 8, r), lambda b, g, n: (b, g, 0, tile(n))),
        ]

    out_shape = jax.ShapeDtypeStruct((t, DN_V), F32)
    return pl.pallas_call(
        _dn_scan_kernel,
        grid=(batch, DN_K_HEADS, nt),
        in_specs=specs(row_f, lambda n: n) + specs(row_b, lambda n: nt - 1 - n),
        out_specs=[
            pl.BlockSpec((r, 2 * DN_DIM), lambda b, g, n: (row_f(b, g, n), g)),
            pl.BlockSpec((r, 2 * DN_DIM), lambda b, g, n: (row_b(b, g, n), g)),
        ],
        out_shape=[out_shape, out_shape],
        scratch_shapes=[pltpu.VMEM((4, DN_DIM, DN_DIM), F32)],
        compiler_params=_cparams(("parallel", "parallel", "arbitrary")),
        name="dn_scan",
    )(dn_qkv, dn_qkv, dn_qkv, gates, gates_rows, dn_qkv, dn_qkv, dn_qkv, gates, gates_rows)


def _dn_final_kernel(of_ref, ob_ref, z_ref, nw_ref, o_ref):
    o = of_ref[...] + ob_ref[...]
    nw = nw_ref[...]
    outs = []
    for s in range(o.shape[1] // DN_DIM):
        os_ = o[:, s * DN_DIM:(s + 1) * DN_DIM]
        outs.append(os_ * lax.rsqrt(jnp.mean(os_ * os_, axis=-1, keepdims=True) + EPS) * nw)
    z = z_ref[...].astype(F32)
    o_ref[...] = (jnp.concatenate(outs, axis=1) * (z * _sigmoid(z))).astype(o_ref.dtype)


def _dn_final(o_f, o_b, proj, norm_w, tr, tc):
    t = o_f.shape[0]
    zcol0 = OFF_DZ // tc
    return pl.pallas_call(
        _dn_final_kernel,
        grid=(t // tr, DN_V // tc),
        in_specs=[
            pl.BlockSpec((tr, tc), lambda i, j: (i, j)),
            pl.BlockSpec((tr, tc), lambda i, j: (i, j)),
            pl.BlockSpec((tr, tc), lambda i, j: (i, zcol0 + j)),
            pl.BlockSpec((1, DN_DIM), lambda i, j: (0, 0)),
        ],
        out_specs=pl.BlockSpec((tr, tc), lambda i, j: (i, j)),
        out_shape=jax.ShapeDtypeStruct((t, DN_V), BF16),
        compiler_params=_cparams(("parallel", "parallel")),
        name="dn_final",
    )(o_f, o_b, proj, norm_w)


def _merge_kernel(ret_ref, dn_ref, wr_ref, wd_ref, gr_ref, gd_ref, o_ref):
    a = _dot(ret_ref[...], wr_ref[...])
    b = _dot(dn_ref[...], wd_ref[...])
    o_ref[...] = (_sigmoid(gr_ref[...].astype(F32)) * a + _sigmoid(gd_ref[...].astype(F32)) * b).astype(o_ref.dtype)


def _merge(ret, dn, w_br, w_bd, proj, tm, tn):
    t = ret.shape[0]
    return pl.pallas_call(
        _merge_kernel,
        grid=(t // tm, D_MODEL // tn),
        in_specs=[
            pl.BlockSpec((tm, RET_V), lambda i, j: (i, 0)),
            pl.BlockSpec((tm, DN_V), lambda i, j: (i, 0)),
            pl.BlockSpec((RET_V, tn), lambda i, j: (0, j)),
            pl.BlockSpec((DN_V, tn), lambda i, j: (0, j)),
            pl.BlockSpec((tm, tn), lambda i, j: (i, OFF_GR // tn + j)),
            pl.BlockSpec((tm, tn), lambda i, j: (i, OFF_GD // tn + j)),
        ],
        out_specs=pl.BlockSpec((tm, tn), lambda i, j: (i, j)),
        out_shape=jax.ShapeDtypeStruct((t, D_MODEL), BF16),
        compiler_params=_cparams(("parallel", "parallel")),
        name="branch_merge",
    )(ret, dn, w_br, w_bd, proj, proj)


def _out_proj_kernel(m_ref, w_ref, x_ref, o_ref):
    o_ref[...] = x_ref[...] + _dot(m_ref[...], w_ref[...])


def _out_proj(merged, w_out, x, tm, tn):
    t = merged.shape[0]
    return pl.pallas_call(
        _out_proj_kernel,
        grid=(t // tm, D_MODEL // tn),
        in_specs=[
            pl.BlockSpec((tm, D_MODEL), lambda i, j: (i, 0)),
            pl.BlockSpec((D_MODEL, tn), lambda i, j: (0, j)),
            pl.BlockSpec((tm, tn), lambda i, j: (i, j)),
        ],
        out_specs=pl.BlockSpec((tm, tn), lambda i, j: (i, j)),
        out_shape=jax.ShapeDtypeStruct((t, D_MODEL), F32),
        compiler_params=_cparams(("parallel", "parallel")),
        name="out_proj",
    )(merged, w_out, x)


def _router_kernel(h_ref, nw_ref, wr_ref, br_ref, xn_ref, route_ref, count_ref, carry_ref):
    tr = h_ref.shape[0]

    @pl.when(pl.program_id(0) == 0)
    def _():
        carry_ref[...] = jnp.zeros_like(carry_ref)

    h = h_ref[...]
    xn = h * lax.rsqrt(jnp.mean(h * h, axis=-1, keepdims=True) + EPS) * nw_ref[...]
    xn_ref[...] = _pack_bf16_pairs(xn)
    logits = jnp.dot(xn, wr_ref[...], preferred_element_type=F32, precision=lax.Precision.HIGHEST) + br_ref[...]
    lane = lax.broadcasted_iota(jnp.int32, (tr, LANES), 1)
    neg = jnp.float32(-jnp.inf)
    work = jnp.where(lane < N_EXPERTS, logits, neg)
    vals, idxs = [], []
    onehot = jnp.zeros((tr, LANES), F32)
    for _ in range(TOP_K):
        mx = jnp.max(work, axis=-1, keepdims=True)
        ix = jnp.min(jnp.where(work == mx, lane, LANES), axis=-1, keepdims=True)
        sel = lane == ix
        onehot = jnp.where(sel, 1.0, onehot)
        work = jnp.where(sel, neg, work)
        vals.append(mx)
        idxs.append(ix)
    exps = [jnp.exp(v - vals[0]) for v in vals]
    denom = exps[0] + exps[1] + exps[2] + exps[3]
    ri = lax.broadcasted_iota(jnp.int32, (tr, tr), 0)
    rj = lax.broadcasted_iota(jnp.int32, (tr, tr), 1)
    lower = (ri > rj).astype(BF16)
    carry = carry_ref[0:1, :]
    rank = carry + _dot(lower, onehot.astype(BF16))
    out = jnp.zeros((tr, LANES), F32)
    for kk in range(TOP_K):
        rk = jnp.sum(jnp.where(lane == idxs[kk], rank, 0.0), axis=-1, keepdims=True)
        out = jnp.where(lane == kk, idxs[kk].astype(F32), out)
        out = jnp.where(lane == TOP_K + kk, exps[kk] / denom, out)
        out = jnp.where(lane == 2 * TOP_K + kk, rk, out)
    route_ref[...] = out
    new_carry = carry + jnp.sum(onehot, axis=0, keepdims=True)
    carry_ref[...] = jnp.broadcast_to(new_carry, carry_ref.shape)
    count_ref[...] = jnp.broadcast_to(new_carry, count_ref.shape)


def _router(h, nw, w_router_pad, b_router_pad, tr):
    t, d = h.shape
    return pl.pallas_call(
        _router_kernel,
        grid=(t // tr,),
        in_specs=[
            pl.BlockSpec((tr, d), lambda i: (i, 0)),
            pl.BlockSpec((1, d), lambda i: (0, 0)),
            pl.BlockSpec((d, LANES), lambda i: (0, 0)),
            pl.BlockSpec((1, LANES), lambda i: (0, 0)),
        ],
        out_specs=[
            pl.BlockSpec((tr, d // 2), lambda i: (i, 0)),
            pl.BlockSpec((tr, LANES), lambda i: (i, 0)),
            pl.BlockSpec((8, LANES), lambda i: (0, 0)),
        ],
        out_shape=[
            jax.ShapeDtypeStruct((t, d // 2), jnp.uint32),
            jax.ShapeDtypeStruct((t, LANES), F32),
            jax.ShapeDtypeStruct((8, LANES), F32),
        ],
        scratch_shapes=[pltpu.VMEM((8, LANES), F32)],
        compiler_params=_cparams(("arbitrary",)),
        name="moe_router",
    )(h, nw, w_router_pad, b_router_pad)


def _row_copy(src_hbm, dst_ref, sem, src_row, dst_row):
    return pltpu.make_async_copy(src_hbm.at[pl.ds(src_row, 1), :], dst_ref.at[pl.ds(dst_row, 1), :], sem)


def _dispatch_kernel(idx_ref, src_hbm, o_ref, sem):
    tr = o_ref.shape[0]

    def start(r, carry):
        _row_copy(src_hbm, o_ref, sem, idx_ref[r], r).start()
        return carry

    lax.fori_loop(0, tr, start, 0, unroll=8)

    def wait(r, carry):
        _row_copy(src_hbm, o_ref, sem, 0, r).wait()
        return carry

    lax.fori_loop(0, tr, wait, 0, unroll=8)


def _dispatch(buf_tok, src, tr):
    n_rows = buf_tok.shape[0]
    d = src.shape[1]
    return pl.pallas_call(
        _dispatch_kernel,
        grid=(n_rows // tr,),
        in_specs=[
            pl.BlockSpec((tr,), lambda i: (i,), memory_space=pltpu.SMEM),
            pl.BlockSpec(memory_space=pl.ANY),
        ],
        out_specs=pl.BlockSpec((tr, d), lambda i: (i, 0)),
        out_shape=jax.ShapeDtypeStruct((n_rows, d), src.dtype),
        scratch_shapes=[pltpu.SemaphoreType.DMA(())],
        compiler_params=_cparams(("arbitrary",)),
        name="moe_dispatch",
    )(buf_tok, src)


def _block_copy(src_ref, dst_hbm, sem, src_blk, dst_blk):
    return pltpu.make_async_copy(src_ref.at[pl.ds(src_blk * MOE_BLOCK, MOE_BLOCK), :],
                                 dst_hbm.at[pl.ds(pl.multiple_of(dst_blk * MOE_BLOCK, MOE_BLOCK), MOE_BLOCK), :],
                                 sem)


def _expert_kernel(ce_ref, cs_ref, cn_ref, cw_ref, x_ref, wg_ref, bg_ref, wu_ref, bu_ref, wd_ref, bd_ref,
                   y_hbm, xb_ref, acc_ref, ybuf_ref, sem, *, n_blocks):
    i = pl.program_id(0)
    j = pl.program_id(1)
    nf = pl.num_programs(1)
    c = cn_ref[i]
    start = cs_ref[i]
    win_off = (start - cw_ref[i]) * MOE_BLOCK
    last = j == nf - 1

    def run(row0, m):
        rows = pl.ds(row0, m)

        @pl.when(j == 0)
        def _():
            lo, hi = _unpack_bf16_pairs(x_ref[pl.ds(pl.multiple_of(win_off + row0, MOE_BLOCK), m), :])
            xb_ref[rows, :] = jnp.concatenate([lo.astype(BF16), hi.astype(BF16)], axis=1)

        xb = xb_ref[rows, :]
        gate = _dot(xb, wg_ref[...].astype(BF16)) + bg_ref[...]
        up = _dot(xb, wu_ref[...].astype(BF16)) + bu_ref[...]
        gate = jnp.minimum(gate, SWIGLU_LIMIT)
        up = jnp.clip(up, -SWIGLU_LIMIT, SWIGLU_LIMIT)
        hid = (up + 1.0) * gate * _sigmoid(SWIGLU_ALPHA * gate)
        part = _dot(hid.astype(BF16), wd_ref[...].astype(BF16))

        @pl.when(j == 0)
        def _():
            acc_ref[rows, :] = part + bd_ref[...]

        @pl.when((j > 0) & jnp.logical_not(last))
        def _():
            acc_ref[rows, :] += part

        @pl.when(last)
        def _():
            ybuf_ref[rows, :] = _pack_bf16_pairs(acc_ref[rows, :] + part)

    @pl.when(c == 4)
    def _():
        run(0, 4 * MOE_BLOCK)

    @pl.when((c == 2) | (c == 3))
    def _():
        run(0, 2 * MOE_BLOCK)

    @pl.when(c == 3)
    def _():
        run(2 * MOE_BLOCK, MOE_BLOCK)

    @pl.when(c == 1)
    def _():
        run(0, MOE_BLOCK)

    @pl.when(last & (c == 0))
    def _():
        ybuf_ref[...] = jnp.zeros_like(ybuf_ref)

    @pl.when(last)
    def _():
        def writes(r):
            return (r < c) | ((c == 0) & (start + r < n_blocks))

        for r in range(MOE_CHUNK):
            @pl.when(writes(r))
            def _():
                _block_copy(ybuf_ref, y_hbm, sem, r, start + r).start()

        for r in range(MOE_CHUNK):
            @pl.when(writes(r))
            def _():
                _block_copy(ybuf_ref, y_hbm, sem, r, start + r).wait()


def _experts(c_expert, c_start, c_count, c_window, x_sorted, w_gate, b_gate, w_up, b_up, w_down, b_down, tf):
    n_rows, dp = x_sorted.shape
    d = 2 * dp
    nf = D_FF // tf
    n_chunks = c_expert.shape[0]
    rows = MOE_CHUNK * MOE_BLOCK

    def tile(i, j, cn):
        return jnp.where(cn[i] > 0, j, nf - 1)

    grid_spec = pltpu.PrefetchScalarGridSpec(
        num_scalar_prefetch=4,
        grid=(n_chunks, nf),
        in_specs=[
            pl.BlockSpec((pl.Element(rows), pl.Element(dp)),
                         lambda i, j, ce, cs, cn, cw: (cw[i] * MOE_BLOCK, 0)),
            pl.BlockSpec((None, d, tf), lambda i, j, ce, cs, cn, cw: (ce[i], 0, tile(i, j, cn))),
            pl.BlockSpec((None, 1, tf), lambda i, j, ce, cs, cn, cw: (ce[i], 0, tile(i, j, cn))),
            pl.BlockSpec((None, d, tf), lambda i, j, ce, cs, cn, cw: (ce[i], 0, tile(i, j, cn))),
            pl.BlockSpec((None, 1, tf), lambda i, j, ce, cs, cn, cw: (ce[i], 0, tile(i, j, cn))),
            pl.BlockSpec((None, tf, d), lambda i, j, ce, cs, cn, cw: (ce[i], tile(i, j, cn), 0)),
            pl.BlockSpec((None, 1, d), lambda i, j, ce, cs, cn, cw: (ce[i], 0, 0)),
        ],
        out_specs=pl.BlockSpec(memory_space=pl.ANY),
        scratch_shapes=[pltpu.VMEM((rows, d), BF16), pltpu.VMEM((rows, d), F32),
                        pltpu.VMEM((rows, dp), jnp.uint32), pltpu.SemaphoreType.DMA(())],
    )
    return pl.pallas_call(
        functools.partial(_expert_kernel, n_blocks=n_rows // MOE_BLOCK),
        grid_spec=grid_spec,
        out_shape=jax.ShapeDtypeStruct((n_rows, dp), jnp.uint32),
        compiler_params=pltpu.CompilerParams(dimension_semantics=("arbitrary", "arbitrary"),
                                             vmem_limit_bytes=EXPERT_VMEM_LIMIT),
        name="moe_experts",
    )(c_expert, c_start, c_count, c_window, x_sorted, w_gate, b_gate, w_up, b_up, w_down, b_down)


def _combine_kernel(dest_ref, y_hbm, h_ref, route_ref, nw_ref, o_ref, buf_ref, sem):
    tr = h_ref.shape[0]

    def start(r, carry):
        for kk in range(TOP_K):
            _row_copy(y_hbm, buf_ref.at[kk], sem, dest_ref[r * TOP_K + kk], r).start()
        return carry

    lax.fori_loop(0, tr, start, 0, unroll=2)

    def wait(r, carry):
        for kk in range(TOP_K):
            _row_copy(y_hbm, buf_ref.at[kk], sem, 0, r).wait()
        return carry

    lax.fori_loop(0, tr, wait, 0, unroll=2)

    route = route_ref[...]
    h = h_ref[...]
    dp = h.shape[1] // 2
    acc_lo = h[:, :dp]
    acc_hi = h[:, dp:]
    for kk in range(TOP_K):
        w = route[:, TOP_K + kk:TOP_K + kk + 1]
        lo, hi = _unpack_bf16_pairs(buf_ref[kk])
        acc_lo = acc_lo + w * lo
        acc_hi = acc_hi + w * hi
    acc = jnp.concatenate([acc_lo, acc_hi], axis=1)
    o_ref[...] = acc * lax.rsqrt(jnp.mean(acc * acc, axis=-1, keepdims=True) + EPS) * nw_ref[...]


def _combine(dest, y_sorted, h, route, nw, tr):
    t, d = h.shape
    return pl.pallas_call(
        _combine_kernel,
        grid=(t // tr,),
        in_specs=[
            pl.BlockSpec((tr * TOP_K,), lambda i: (i,), memory_space=pltpu.SMEM),
            pl.BlockSpec(memory_space=pl.ANY),
            pl.BlockSpec((tr, d), lambda i: (i, 0)),
            pl.BlockSpec((tr, LANES), lambda i: (i, 0)),
            pl.BlockSpec((1, d), lambda i: (0, 0)),
        ],
        out_specs=pl.BlockSpec((tr, d), lambda i: (i, 0)),
        out_shape=jax.ShapeDtypeStruct((t, d), F32),
        scratch_shapes=[pltpu.VMEM((TOP_K, tr, d // 2), jnp.uint32), pltpu.SemaphoreType.DMA(())],
        compiler_params=_cparams(("arbitrary",)),
        name="moe_combine",
    )(dest, y_sorted, h, route, nw)


def _deinterleave_perm():
    half = RET_QK_DIM // 2
    per_head = jnp.concatenate([jnp.arange(half) * 2, jnp.arange(half) * 2 + 1])
    return (jnp.arange(RET_HEADS)[:, None] * RET_QK_DIM + per_head[None, :]).reshape(-1)


def _split_w_in(w_in):
    o = 0
    parts = {}
    for name, width in (("rq", RET_QK), ("rk", RET_QK), ("rv", RET_V), ("rg", RET_V), ("dqkv", DN_CONV_CH),
                        ("dz", DN_V), ("small", 4 * DN_V_HEADS), ("gr", D_MODEL), ("gd", D_MODEL)):
        parts[name] = w_in[:, o:o + width]
        o += width
    perm = _deinterleave_perm()
    w_main = jnp.concatenate(
        [parts["rq"][:, perm], parts["rk"][:, perm], parts["rv"], parts["rg"], parts["dqkv"], parts["dz"],
         parts["gr"], parts["gd"]], axis=1).astype(BF16)
    w_small = jnp.pad(parts["small"], ((0, 0), (0, LANES - 4 * DN_V_HEADS))).astype(BF16)
    return w_main, w_small


def _gate_rows(gates, batch, seq):
    nh = DN_V_HEADS
    rows = gates[:, 2 * nh:4 * nh].reshape(batch, seq, 2, DN_K_HEADS, 2)
    rows = rows.transpose(0, 3, 2, 4, 1).reshape(batch, DN_K_HEADS, 4, seq)
    return jnp.concatenate([rows, jnp.zeros_like(rows)], axis=2)


def _token_mixer(x2, batch, seq, norm1_w, w_in, conv_w, a_log_f, a_log_b, dt_f, dt_b, dn_norm_w,
                 w_br, w_bd, w_out):
    t = batch * seq
    w_main, w_small = _split_w_in(w_in)
    nw1 = norm1_w.reshape(1, D_MODEL)
    proj = _norm_matmul(x2, nw1, w_main, BF16, tm=min(2048, t), tn=512)

    zeros = jnp.zeros((2 * DN_V_HEADS,), F32)
    pad = jnp.zeros((LANES - 4 * DN_V_HEADS,), F32)
    alog_row = jnp.concatenate([zeros, a_log_f, a_log_b, pad]).reshape(1, LANES)
    dtb_row = jnp.concatenate([zeros, dt_f, dt_b, pad]).reshape(1, LANES)
    gates = _gate_proj(x2, nw1, w_small, alog_row, dtb_row, tm=min(512, t))

    angle = 1.0 / (ROPE_BASE ** jnp.linspace(0.0, 1.0, RET_QK_DIM // 2, dtype=F32))
    theta = jnp.arange(seq, dtype=F32)[:, None] * angle[None, :]
    log_g = jnp.log(1.0 - 2.0 ** (-5.0 - jnp.arange(2 * RET_HEADS, dtype=F32)))
    ret = _retention(proj, jnp.cos(theta), jnp.sin(theta), log_g, batch, seq)

    dn_qkv = _dn_prep(proj, conv_w, seq, tr=min(512, seq), tc=512)
    o_f, o_b = _dn_scan(dn_qkv, gates, _gate_rows(gates, batch, seq), batch, seq)
    dn = _dn_final(o_f, o_b, proj, dn_norm_w.reshape(1, DN_DIM), tr=min(512, t), tc=512)

    merged = _merge(ret, dn, w_br.astype(BF16), w_bd.astype(BF16), proj, tm=min(1024, t), tn=512)
    return _out_proj(merged, w_out.astype(BF16), x2, tm=min(1024, t), tn=512)


def _moe(h, norm2_w, w_router, b_router, w_gate, b_gate, w_up, b_up, w_down, b_down, norm_f_w):
    t = h.shape[0]
    wr = jnp.pad(w_router, ((0, 0), (0, LANES - N_EXPERTS)))
    br = jnp.pad(b_router, (0, LANES - N_EXPERTS)).reshape(1, LANES)
    xn, route, counts = _router(h, norm2_w.reshape(1, D_MODEL), wr, br, tr=min(256, t))

    top_idx = route[:, :TOP_K].astype(jnp.int32)
    rank = route[:, 2 * TOP_K:3 * TOP_K].astype(jnp.int32)
    counts = counts[0, :N_EXPERTS].astype(jnp.int32)
    n_assign = t * TOP_K
    n_rows = ((n_assign + N_EXPERTS * (MOE_BLOCK - 1) + MOE_BLOCK - 1) // MOE_BLOCK) * MOE_BLOCK
    n_blocks = n_rows // MOE_BLOCK
    padded = ((counts + MOE_BLOCK - 1) // MOE_BLOCK) * MOE_BLOCK
    pad_ends = jnp.cumsum(padded)
    pad_starts = pad_ends - padded
    dest = (pad_starts[top_idx] + rank).reshape(-1)
    tok = jnp.repeat(jnp.arange(t, dtype=jnp.int32), TOP_K)
    buf_tok = jnp.zeros((n_rows,), jnp.int32).at[dest].set(tok)
    nb_e = padded // MOE_BLOCK
    blk_start_e = pad_starts // MOE_BLOCK
    nc_e = (nb_e + MOE_CHUNK - 1) // MOE_CHUNK
    c_end_e = jnp.cumsum(nc_e)
    n_chunks = N_EXPERTS + n_blocks // MOE_CHUNK
    cid = jnp.arange(n_chunks, dtype=jnp.int32)
    n_live = c_end_e[-1]
    cid_c = jnp.minimum(cid, n_live - 1)
    e_of = jnp.sum((cid_c[:, None] >= c_end_e[None, :]).astype(jnp.int32), axis=1)
    local = cid_c - (c_end_e - nc_e)[e_of]
    c_start = blk_start_e[e_of] + local * MOE_CHUNK
    c_count = jnp.clip(nb_e[e_of] - local * MOE_CHUNK, 0, MOE_CHUNK)
    c_window = jnp.minimum(c_start, n_blocks - MOE_CHUNK)
    idle = cid >= n_live
    n_used = pad_ends[-1] // MOE_BLOCK
    c_start = jnp.where(idle, n_used + (cid - n_live) * MOE_CHUNK, c_start).astype(jnp.int32)
    c_count = jnp.where(idle, 0, c_count).astype(jnp.int32)

    x_sorted = _dispatch(buf_tok, xn, tr=4 * MOE_BLOCK)
    y_sorted = _experts(e_of.astype(jnp.int32), c_start, c_count, c_window.astype(jnp.int32), x_sorted,
                        w_gate, b_gate.reshape(N_EXPERTS, 1, D_FF),
                        w_up, b_up.reshape(N_EXPERTS, 1, D_FF),
                        w_down, b_down.reshape(N_EXPERTS, 1, D_MODEL), tf=256)
    return _combine(dest, y_sorted, h, route, norm_f_w.reshape(1, D_MODEL), tr=min(512, t))


def kernel(x, norm1_w, w_in, conv_w, dn_a_log_f, dn_a_log_b, dn_dt_bias_f, dn_dt_bias_b, dn_norm_w, w_branch_ret, w_branch_dn, w_out, norm2_w, w_router, b_router, w_gate, b_gate, w_up, b_up, w_down, b_down, norm_f_w):
    batch, seq, d = x.shape
    assert norm1_w.shape[0] == 1, "one layer"
    x2 = x.reshape(batch * seq, d)
    h = _token_mixer(x2, batch, seq, norm1_w[0], w_in[0], conv_w[0], dn_a_log_f[0], dn_a_log_b[0],
                     dn_dt_bias_f[0], dn_dt_bias_b[0], dn_norm_w[0], w_branch_ret[0], w_branch_dn[0], w_out[0])
    out = _moe(h, norm2_w[0], w_router[0], b_router[0], w_gate[0], b_gate[0], w_up[0], b_up[0],
               w_down[0], b_down[0], norm_f_w)
    return out.reshape(batch, seq, d)
```

```python
import functools

import jax
import jax.numpy as jnp
from jax import lax
from jax.experimental import pallas as pl
from jax.experimental.pallas import tpu as pltpu

F32 = jnp.float32
BF16 = jnp.bfloat16

D_MODEL = 2048
RET_HEADS = 4
RET_QK_DIM = 256
RET_V_DIM = 512
ROPE_BASE = 10000.0
DN_K_HEADS = 8
DN_V_HEADS = 16
DN_DIM = 128
DN_CONV = 4
DN_CHUNK = 64
N_EXPERTS = 32
TOP_K = 4
D_FF = 2048
SWIGLU_LIMIT = 7.0
SWIGLU_ALPHA = 1.702
EPS = 1e-6

RET_QK = RET_HEADS * RET_QK_DIM
RET_V = RET_HEADS * RET_V_DIM
DN_QK = DN_K_HEADS * DN_DIM
DN_V = DN_V_HEADS * DN_DIM
DN_CONV_CH = 2 * DN_QK + DN_V

OFF_RQ = 0
OFF_RK = OFF_RQ + RET_QK
OFF_RV = OFF_RK + RET_QK
OFF_RG = OFF_RV + RET_V
OFF_DQKV = OFF_RG + RET_V
OFF_DZ = OFF_DQKV + DN_CONV_CH
OFF_GR = OFF_DZ + DN_V
OFF_GD = OFF_GR + D_MODEL
N_MAIN = OFF_GD + D_MODEL
LANES = 128

RET_CHUNK = 256
DN_TILE = 256
DN_GROUPS_PER_STEP = 2
MOE_BLOCK = 256
MOE_CHUNK = 4
VMEM_LIMIT = 56 * 1024 * 1024
EXPERT_VMEM_LIMIT = 60 * 1024 * 1024


def _cparams(sem):
    return pltpu.CompilerParams(dimension_semantics=sem, vmem_limit_bytes=VMEM_LIMIT)


def _dot(a, b):
    return jnp.dot(a, b, preferred_element_type=F32)


def _dot_nt(a, b):
    return lax.dot_general(a, b, (((1,), (1,)), ((), ())), preferred_element_type=F32)


def _dot_tn(a, b):
    return lax.dot_general(a, b, (((0,), (0,)), ((), ())), preferred_element_type=F32)


def _sigmoid(x):
    return 1.0 / (1.0 + jnp.exp(-x))


def _pack_bf16_pairs(x):
    n = x.shape[1] // 2
    bits = lax.bitcast_convert_type(x.astype(BF16).astype(F32), jnp.uint32)
    return (bits[:, n:] & jnp.uint32(0xFFFF0000)) | (bits[:, :n] >> 16)


def _unpack_bf16_pairs(p):
    lo = lax.bitcast_convert_type(p << 16, F32)
    hi = lax.bitcast_convert_type(p & jnp.uint32(0xFFFF0000), F32)
    return lo, hi


def _norm_matmul_kernel(x_ref, nw_ref, w_ref, o_ref, xn_ref):
    @pl.when(pl.program_id(1) == 0)
    def _():
        x = x_ref[...]
        ms = jnp.mean(x * x, axis=-1, keepdims=True)
        xn_ref[...] = (x * lax.rsqrt(ms + EPS) * nw_ref[...]).astype(BF16)

    o_ref[...] = _dot(xn_ref[...], w_ref[...]).astype(o_ref.dtype)


def _norm_matmul(x, nw, w, out_dtype, tm, tn):
    t, d = x.shape
    n = w.shape[1]
    return pl.pallas_call(
        _norm_matmul_kernel,
        grid=(t // tm, n // tn),
        in_specs=[
            pl.BlockSpec((tm, d), lambda i, j: (i, 0)),
            pl.BlockSpec((1, d), lambda i, j: (0, 0)),
            pl.BlockSpec((d, tn), lambda i, j: (0, j)),
        ],
        out_specs=pl.BlockSpec((tm, tn), lambda i, j: (i, j)),
        out_shape=jax.ShapeDtypeStruct((t, n), out_dtype),
        scratch_shapes=[pltpu.VMEM((tm, d), BF16)],
        compiler_params=_cparams(("parallel", "arbitrary")),
        name="norm_in_proj",
    )(x, nw, w)


def _gate_proj_kernel(x_ref, nw_ref, w_ref, alog_ref, dtb_ref, o_ref):
    x = x_ref[...]
    ms = jnp.mean(x * x, axis=-1, keepdims=True)
    xn = (x * lax.rsqrt(ms + EPS) * nw_ref[...]).astype(BF16)
    p = _dot(xn, w_ref[...])
    lane = lax.broadcasted_iota(jnp.int32, p.shape, 1)
    beta = _sigmoid(p)
    z = p + dtb_ref[...]
    softplus = jnp.maximum(z, 0.0) + jnp.log(1.0 + jnp.exp(-jnp.abs(z)))
    g = -jnp.exp(alog_ref[...]) * softplus
    tm = p.shape[0]
    ci = lax.broadcasted_iota(jnp.int32, (tm, tm), 0)
    cj = lax.broadcasted_iota(jnp.int32, (tm, tm), 1)
    same = _blk(ci, DN_CHUNK) == _blk(cj, DN_CHUNK)

    def chunk_sum(mask):
        return jnp.dot(mask.astype(F32), g, preferred_element_type=F32, precision=lax.Precision.HIGHEST)

    csum_f = chunk_sum(same & (ci >= cj))
    csum_b = chunk_sum(same & (ci <= cj))
    ctot = pltpu.roll(chunk_sum(same), 2 * DN_V_HEADS, 1)
    nh = DN_V_HEADS
    o_ref[...] = jnp.where(lane < 2 * nh, beta,
                           jnp.where(lane < 3 * nh, csum_f,
                                     jnp.where(lane < 4 * nh, csum_b,
                                               jnp.where(lane < 6 * nh, ctot, 0.0))))


def _gate_proj(x, nw, w_small, alog_row, dtb_row, tm):
    t, d = x.shape
    return pl.pallas_call(
        _gate_proj_kernel,
        grid=(t // tm,),
        in_specs=[
            pl.BlockSpec((tm, d), lambda i: (i, 0)),
            pl.BlockSpec((1, d), lambda i: (0, 0)),
            pl.BlockSpec((d, LANES), lambda i: (0, 0)),
            pl.BlockSpec((1, LANES), lambda i: (0, 0)),
            pl.BlockSpec((1, LANES), lambda i: (0, 0)),
        ],
        out_specs=pl.BlockSpec((tm, LANES), lambda i: (i, 0)),
        out_shape=jax.ShapeDtypeStruct((t, LANES), F32),
        compiler_params=_cparams(("parallel",)),
        name="dn_gate_proj",
    )(x, nw, w_small, alog_row, dtb_row)


def _rotary(t, cos, sin):
    t = t.astype(F32)
    half = RET_QK_DIM // 2
    t1 = t[:, :half]
    t2 = t[:, half:]
    return jnp.concatenate([t1 * cos - t2 * sin, t2 * cos + t1 * sin], axis=1)


def _ret_heads(q_ref, k_ref, v_ref, cos_ref, sin_ref):
    cos = cos_ref[...]
    sin = sin_ref[...]
    dk, dv = RET_QK_DIM, RET_V_DIM
    qs = [_rotary(q_ref[:, h * dk:(h + 1) * dk], cos, sin) for h in range(RET_HEADS)]
    ks = [_rotary(k_ref[:, h * dk:(h + 1) * dk], cos, sin) * (dk ** -0.5) for h in range(RET_HEADS)]
    vs = [v_ref[:, h * dv:(h + 1) * dv] for h in range(RET_HEADS)]
    return qs, ks, vs


def _ret_fwd_kernel(lg_ref, q_ref, k_ref, v_ref, cos_ref, sin_ref, o_ref, state_ref, dmask_ref):
    c = RET_CHUNK
    heads = range(RET_HEADS)
    lg_f = [lg_ref[2 * h] for h in heads]
    lg_b = [lg_ref[2 * h + 1] for h in heads]

    @pl.when(pl.program_id(1) == 0)
    def _():
        state_ref[...] = jnp.zeros_like(state_ref)
        i = lax.broadcasted_iota(jnp.int32, (c, c), 0)
        j = lax.broadcasted_iota(jnp.int32, (c, c), 1)
        d = (i - j).astype(F32)
        for h in heads:
            dmask_ref[h] = jnp.where(d >= 0, jnp.exp(d * lg_f[h]), jnp.exp(-d * lg_b[h]))

    qs, ks, vs = _ret_heads(q_ref, k_ref, v_ref, cos_ref, sin_ref)
    idx = lax.broadcasted_iota(jnp.int32, (c, 1), 0).astype(F32)
    scores = [_dot_nt(qs[h].astype(BF16), ks[h].astype(BF16)) * dmask_ref[h] for h in heads]
    states = [state_ref[h] for h in heads]
    inter = [_dot((qs[h] * jnp.exp((idx + 1.0) * lg_f[h])).astype(BF16), states[h].astype(BF16)) for h in heads]
    outs = [_dot(scores[h].astype(BF16), vs[h]) + inter[h] for h in heads]
    for h in heads:
        c_dec = jnp.exp(jnp.full((1, 1), c, F32) * lg_f[h])
        k_dec = jnp.exp((c - 1.0 - idx) * lg_f[h])
        state_ref[h] = states[h] * c_dec + _dot_tn((ks[h] * k_dec).astype(BF16), vs[h])
    o_ref[...] = jnp.concatenate(outs, axis=1)


def _ret_bwd_kernel(lg_ref, q_ref, k_ref, v_ref, cos_ref, sin_ref, oacc_ref, rg_ref, o_ref, state_ref):
    c = RET_CHUNK
    heads = range(RET_HEADS)
    dv = RET_V_DIM
    lg_b = [lg_ref[2 * h + 1] for h in heads]

    @pl.when(pl.program_id(1) == 0)
    def _():
        state_ref[...] = jnp.zeros_like(state_ref)

    qs, ks, vs = _ret_heads(q_ref, k_ref, v_ref, cos_ref, sin_ref)
    idx = lax.broadcasted_iota(jnp.int32, (c, 1), 0).astype(F32)
    states = [state_ref[h] for h in heads]
    inter = [_dot((qs[h] * jnp.exp((c - idx) * lg_b[h])).astype(BF16), states[h].astype(BF16)) for h in heads]
    for h in heads:
        c_dec = jnp.exp(jnp.full((1, 1), c, F32) * lg_b[h])
        k_dec = jnp.exp(idx * lg_b[h])
        state_ref[h] = states[h] * c_dec + _dot_tn((ks[h] * k_dec).astype(BF16), vs[h])
    outs = []
    for h in heads:
        o = oacc_ref[:, h * dv:(h + 1) * dv] + inter[h]
        o = o * lax.rsqrt(jnp.mean(o * o, axis=-1, keepdims=True) + EPS)
        rg = rg_ref[:, h * dv:(h + 1) * dv].astype(F32)
        outs.append((o * (rg * _sigmoid(rg))).astype(o_ref.dtype))
    o_ref[...] = jnp.concatenate(outs, axis=1)


def _retention(proj, cos, sin, log_g, batch, seq):
    c = RET_CHUNK
    nc = seq // c
    t = batch * seq
    half = RET_QK_DIM // 2
    smem = pl.BlockSpec(memory_space=pltpu.SMEM)
    state = pltpu.VMEM((RET_HEADS, RET_QK_DIM, RET_V_DIM), F32)

    def row_f(b, n):
        return b * nc + n

    def row_b(b, n):
        return b * nc + (nc - 1 - n)

    def specs(row):
        return [
            pl.BlockSpec((c, RET_QK), lambda b, n: (row(b, n), OFF_RQ // RET_QK)),
            pl.BlockSpec((c, RET_QK), lambda b, n: (row(b, n), OFF_RK // RET_QK)),
            pl.BlockSpec((c, RET_V), lambda b, n: (row(b, n), OFF_RV // RET_V)),
        ]

    o_acc = pl.pallas_call(
        _ret_fwd_kernel,
        grid=(batch, nc),
        in_specs=[smem] + specs(row_f) + [
            pl.BlockSpec((c, half), lambda b, n: (n, 0)),
            pl.BlockSpec((c, half), lambda b, n: (n, 0)),
        ],
        out_specs=pl.BlockSpec((c, RET_V), lambda b, n: (row_f(b, n), 0)),
        out_shape=jax.ShapeDtypeStruct((t, RET_V), F32),
        scratch_shapes=[state, pltpu.VMEM((RET_HEADS, c, c), F32)],
        compiler_params=_cparams(("parallel", "arbitrary")),
        name="retention_fwd",
    )(log_g, proj, proj, proj, cos, sin)

    return pl.pallas_call(
        _ret_bwd_kernel,
        grid=(batch, nc),
        in_specs=[smem] + specs(row_b) + [
            pl.BlockSpec((c, half), lambda b, n: (nc - 1 - n, 0)),
            pl.BlockSpec((c, half), lambda b, n: (nc - 1 - n, 0)),
            pl.BlockSpec((c, RET_V), lambda b, n: (row_b(b, n), 0)),
            pl.BlockSpec((c, RET_V), lambda b, n: (row_b(b, n), OFF_RG // RET_V)),
        ],
        out_specs=pl.BlockSpec((c, RET_V), lambda b, n: (row_b(b, n), 0)),
        out_shape=jax.ShapeDtypeStruct((t, RET_V), BF16),
        scratch_shapes=[state],
        compiler_params=_cparams(("parallel", "arbitrary")),
        name="retention_bwd",
    )(log_g, proj, proj, proj, cos, sin, o_acc, proj)


DN_HALO = 16


def _dn_prep_kernel(x_ref, prev_ref, next_ref, w_ref, o_ref, *, tiles_per_seq):
    tr = x_ref.shape[0]
    i = pl.program_id(0)
    j = pl.program_id(1)
    pos = i % tiles_per_seq
    has_prev = (pos != 0).astype(F32)
    has_next = (pos != tiles_per_seq - 1).astype(F32)
    x = x_ref[...].astype(F32)
    xc = jnp.concatenate(
        [prev_ref[...].astype(F32) * has_prev, x, next_ref[...].astype(F32) * has_next], axis=0)
    n = tr + 2 * DN_HALO
    w = w_ref[...]
    lo, hi = DN_HALO, DN_HALO + tr
    y = (pltpu.roll(xc, 2, 0)[lo:hi] * w[0:1, :] + pltpu.roll(xc, 1, 0)[lo:hi] * w[1:2, :]
         + x * w[2:3, :] + pltpu.roll(xc, n - 1, 0)[lo:hi] * w[3:4, :])
    y = y * _sigmoid(y)
    tc = y.shape[1]
    qk_tiles = (2 * DN_QK) // tc
    q_tiles = DN_QK // tc
    scale = jnp.where(j < q_tiles, DN_DIM ** -0.5, 1.0).astype(F32)
    is_qk = j < qk_tiles
    outs = []
    for s in range(tc // DN_DIM):
        ys = y[:, s * DN_DIM:(s + 1) * DN_DIM]
        rs = lax.rsqrt(jnp.sum(ys * ys, axis=-1, keepdims=True) + EPS) * scale
        outs.append(ys * jnp.where(is_qk, rs, 1.0))
    o_ref[...] = jnp.concatenate(outs, axis=1).astype(o_ref.dtype)


def _dn_prep(proj, conv_w, seq, tr, tc):
    t = proj.shape[0]
    tiles_per_seq = seq // tr
    nrow = t // tr
    col0 = OFF_DQKV // tc
    hb = tr // DN_HALO
    last_halo = t // DN_HALO - 1
    return pl.pallas_call(
        functools.partial(_dn_prep_kernel, tiles_per_seq=tiles_per_seq),
        grid=(nrow, DN_CONV_CH // tc),
        in_specs=[
            pl.BlockSpec((tr, tc), lambda i, j: (i, col0 + j)),
            pl.BlockSpec((DN_HALO, tc), lambda i, j: (jnp.maximum(i * hb - 1, 0), col0 + j)),
            pl.BlockSpec((DN_HALO, tc), lambda i, j: (jnp.minimum((i + 1) * hb, last_halo), col0 + j)),
            pl.BlockSpec((DN_CONV, tc), lambda i, j: (0, j)),
        ],
        out_specs=pl.BlockSpec((tr, tc), lambda i, j: (i, j)),
        out_shape=jax.ShapeDtypeStruct((t, DN_CONV_CH), BF16),
        compiler_params=_cparams(("parallel", "parallel")),
        name="dn_prep",
    )(proj, proj, proj, conv_w)


def _blk(idx, size):
    return lax.shift_right_logical(idx, size.bit_length() - 1)


def _unit_tri_inverse(ms, ci, cj, chunk):
    eye = (ci == cj).astype(F32)
    base = 8
    diag8 = _blk(ci, base) == _blk(cj, base)
    m8 = [jnp.where(diag8, m, 0.0) for m in ms]
    m8b = [m.astype(BF16) for m in m8]
    p2 = [_dot(m, m) for m in m8b]
    p2b = [p.astype(BF16) for p in p2]
    p4 = [_dot(p, p) for p in p2b]
    inv = [_dot((eye - m).astype(BF16), (eye + p).astype(BF16)) for m, p in zip(m8, p2)]
    inv = [_dot(t.astype(BF16), (eye + p).astype(BF16)) for t, p in zip(inv, p4)]
    s = base
    while s < chunk:
        offmask = (_blk(ci, 2 * s) == _blk(cj, 2 * s)) & (_blk(ci, s) != _blk(cj, s))
        invb = [t.astype(BF16) for t in inv]
        left = [_dot(t, jnp.where(offmask, m, 0.0).astype(BF16)).astype(BF16) for t, m in zip(invb, ms)]
        inv = [t - _dot(lf, tb) for t, lf, tb in zip(inv, left, invb)]
        s *= 2
    return inv


def _dn_scan_kernel(qf_ref, kf_ref, vf_ref, gcf_ref, grf_ref,
                    qb_ref, kb_ref, vb_ref, gcb_ref, grb_ref,
                    of_ref, ob_ref, state_ref):
    r = DN_TILE
    c = DN_CHUNK
    nchunk = r // c
    grp = pl.program_id(1)

    @pl.when(pl.program_id(2) == 0)
    def _():
        state_ref[...] = jnp.zeros_like(state_ref)

    ci = lax.broadcasted_iota(jnp.int32, (r, r), 0)
    cj = lax.broadcasted_iota(jnp.int32, (r, r), 1)
    same = _blk(ci, c) == _blk(cj, c)
    lane = lax.broadcasted_iota(jnp.int32, (r, LANES), 1)
    nh = DN_V_HEADS

    def column(gates, lane_idx):
        return jnp.sum(jnp.where(lane == lane_idx, gates, 0.0), axis=-1, keepdims=True)

    chains = []
    for d, (q_ref, k_ref, v_ref, gc_ref, gr_ref) in enumerate(
            ((qf_ref, kf_ref, vf_ref, gcf_ref, grf_ref), (qb_ref, kb_ref, vb_ref, gcb_ref, grb_ref))):
        rev = d == 1
        before_eq = same & ((ci <= cj) if rev else (ci >= cj))
        strict = same & ((ci < cj) if rev else (ci > cj))
        gates = gc_ref[...]
        for gi in range(DN_GROUPS_PER_STEP):
            q = q_ref[:, gi * DN_DIM:(gi + 1) * DN_DIM]
            k = k_ref[:, gi * DN_DIM:(gi + 1) * DN_DIM]
            kf32 = k.astype(F32)
            qf32 = q.astype(F32)
            kk = _dot_nt(k, k)
            qk = _dot_nt(q, k)
            grow = gr_ref[0, gi]
            for hh in range(2):
                head = 2 * (grp * DN_GROUPS_PER_STEP + gi) + hh
                beta_c = column(gates, d * nh + head)
                gc_c = column(gates, (2 + d) * nh + head)
                gt_c = column(gates, (4 + d) * nh + head)
                gc_r = grow[2 * d + hh:2 * d + hh + 1, :]
                decay = jnp.where(before_eq, jnp.exp(jnp.where(before_eq, gc_c - gc_r, 0.0)), 0.0)
                egc = jnp.exp(gc_c)
                vcol = (2 * gi + hh) * DN_DIM
                v = v_ref[:, vcol:vcol + DN_DIM].astype(F32)
                chains.append(dict(
                    rev=rev, sidx=4 * gi + 2 * d + hh,
                    m=jnp.where(strict, beta_c * kk * decay, 0.0),
                    rhs=jnp.concatenate([v * beta_c, kf32 * (beta_c * egc)], axis=1).astype(BF16),
                    qe=(qf32 * egc).astype(BF16),
                    ke=(kf32 * jnp.exp(gt_c - gc_c)).astype(BF16),
                    qkm=(qk * decay).astype(BF16),
                    egt=jnp.exp(gt_c)))
    tinvs = _unit_tri_inverse([ch["m"] for ch in chains], ci, cj, c)
    for ch, tinv in zip(chains, tinvs):
        uw = _dot(tinv.astype(BF16), ch["rhs"])
        ch["u"] = uw[:, :DN_DIM]
        ch["wb"] = uw[:, DN_DIM:].astype(BF16)
        ch["state"] = state_ref[ch["sidx"]]
        ch["o"] = [None] * nchunk
    for step in range(nchunk):
        for ch in chains:
            cc = nchunk - 1 - step if ch["rev"] else step
            lo, hi = cc * c, (cc + 1) * c
            sb = ch["state"].astype(BF16)
            ws = _dot(jnp.concatenate([ch["wb"][lo:hi], ch["qe"][lo:hi]], axis=0), sb)
            v_new = (ch["u"][lo:hi] - ws[:c]).astype(BF16)
            ch["o"][cc] = ws[c:] + _dot(ch["qkm"][lo:hi, lo:hi], v_new)
            ch["state"] = ch["state"] * ch["egt"][lo:lo + 1, :] + _dot_tn(ch["ke"][lo:hi], v_new)
    for ch in chains:
        state_ref[ch["sidx"]] = ch["state"]
    fwd = sorted((ch for ch in chains if not ch["rev"]), key=lambda ch: ch["sidx"])
    bwd = sorted((ch for ch in chains if ch["rev"]), key=lambda ch: ch["sidx"])
    of_ref[...] = jnp.concatenate([jnp.concatenate(ch["o"], axis=0) for ch in fwd], axis=1).astype(of_ref.dtype)
    ob_ref[...] = jnp.concatenate([jnp.concatenate(ch["o"], axis=0) for ch in bwd], axis=1).astype(ob_ref.dtype)


def _dn_scan(dn_qkv, gates, gates_rows, batch, seq):
    r = DN_TILE
    nt = seq // r
    t = batch * seq
    kcol0 = DN_QK // DN_DIM
    vcol0 = (2 * DN_QK) // (2 * DN_DIM)

    def row_f(b, g, n):
        return b * nt + n

    def row_b(b, g, n):
        return b * nt + (nt - 1 - n)

    ng = DN_GROUPS_PER_STEP

    def specs(row, tile):
        return [
            pl.BlockSpec((r, ng * DN_DIM), lambda b, g, n: (row(b, g, n), g)),
            pl.BlockSpec((r, ng * DN_DIM), lambda b, g, n: (row(b, g, n), kcol0 // ng + g)),
            pl.BlockSpec((r, 2 * ng * DN_DIM), lambda b, g, n: (row(b, g, n), vcol0 // ng + g)),
            pl.BlockSpec((r, LANES), lambda b, g, n: (row(b, g, n), 0)),
            pl.BlockSpec((1, ng, 8, r), lambda b, g, n: (b, g, 0, tile(n))),
        ]

    out_shape = jax.ShapeDtypeStruct((t, DN_V), BF16)
    return pl.pallas_call(
        _dn_scan_kernel,
        grid=(batch, DN_K_HEADS // ng, nt),
        in_specs=specs(row_f, lambda n: n) + specs(row_b, lambda n: nt - 1 - n),
        out_specs=[
            pl.BlockSpec((r, 2 * ng * DN_DIM), lambda b, g, n: (row_f(b, g, n), g)),
            pl.BlockSpec((r, 2 * ng * DN_DIM), lambda b, g, n: (row_b(b, g, n), g)),
        ],
        out_shape=[out_shape, out_shape],
        scratch_shapes=[pltpu.VMEM((4 * ng, DN_DIM, DN_DIM), F32)],
        compiler_params=_cparams(("parallel", "parallel", "arbitrary")),
        name="dn_scan",
    )(dn_qkv, dn_qkv, dn_qkv, gates, gates_rows, dn_qkv, dn_qkv, dn_qkv, gates, gates_rows)


def _dn_final_kernel(of_ref, ob_ref, z_ref, nw_ref, o_ref):
    o = of_ref[...].astype(F32) + ob_ref[...].astype(F32)
    nw = nw_ref[...]
    outs = []
    for s in range(o.shape[1] // DN_DIM):
        os_ = o[:, s * DN_DIM:(s + 1) * DN_DIM]
        outs.append(os_ * lax.rsqrt(jnp.mean(os_ * os_, axis=-1, keepdims=True) + EPS) * nw)
    z = z_ref[...].astype(F32)
    o_ref[...] = (jnp.concatenate(outs, axis=1) * (z * _sigmoid(z))).astype(o_ref.dtype)


def _dn_final(o_f, o_b, proj, norm_w, tr, tc):
    t = o_f.shape[0]
    zcol0 = OFF_DZ // tc
    return pl.pallas_call(
        _dn_final_kernel,
        grid=(t // tr, DN_V // tc),
        in_specs=[
            pl.BlockSpec((tr, tc), lambda i, j: (i, j)),
            pl.BlockSpec((tr, tc), lambda i, j: (i, j)),
            pl.BlockSpec((tr, tc), lambda i, j: (i, zcol0 + j)),
            pl.BlockSpec((1, DN_DIM), lambda i, j: (0, 0)),
        ],
        out_specs=pl.BlockSpec((tr, tc), lambda i, j: (i, j)),
        out_shape=jax.ShapeDtypeStruct((t, DN_V), BF16),
        compiler_params=_cparams(("parallel", "parallel")),
        name="dn_final",
    )(o_f, o_b, proj, norm_w)


def _merge_kernel(ret_ref, dn_ref, wr_ref, wd_ref, gr_ref, gd_ref, o_ref):
    a = _dot(ret_ref[...], wr_ref[...])
    b = _dot(dn_ref[...], wd_ref[...])
    o_ref[...] = (_sigmoid(gr_ref[...].astype(F32)) * a + _sigmoid(gd_ref[...].astype(F32)) * b).astype(o_ref.dtype)


def _merge(ret, dn, w_br, w_bd, proj, tm, tn):
    t = ret.shape[0]
    return pl.pallas_call(
        _merge_kernel,
        grid=(t // tm, D_MODEL // tn),
        in_specs=[
            pl.BlockSpec((tm, RET_V), lambda i, j: (i, 0)),
            pl.BlockSpec((tm, DN_V), lambda i, j: (i, 0)),
            pl.BlockSpec((RET_V, tn), lambda i, j: (0, j)),
            pl.BlockSpec((DN_V, tn), lambda i, j: (0, j)),
            pl.BlockSpec((tm, tn), lambda i, j: (i, OFF_GR // tn + j)),
            pl.BlockSpec((tm, tn), lambda i, j: (i, OFF_GD // tn + j)),
        ],
        out_specs=pl.BlockSpec((tm, tn), lambda i, j: (i, j)),
        out_shape=jax.ShapeDtypeStruct((t, D_MODEL), BF16),
        compiler_params=_cparams(("parallel", "parallel")),
        name="branch_merge",
    )(ret, dn, w_br, w_bd, proj, proj)


def _out_proj_kernel(m_ref, w_ref, x_ref, o_ref):
    o_ref[...] = x_ref[...] + _dot(m_ref[...], w_ref[...])


def _out_proj(merged, w_out, x, tm, tn):
    t = merged.shape[0]
    return pl.pallas_call(
        _out_proj_kernel,
        grid=(t // tm, D_MODEL // tn),
        in_specs=[
            pl.BlockSpec((tm, D_MODEL), lambda i, j: (i, 0)),
            pl.BlockSpec((D_MODEL, tn), lambda i, j: (0, j)),
            pl.BlockSpec((tm, tn), lambda i, j: (i, j)),
        ],
        out_specs=pl.BlockSpec((tm, tn), lambda i, j: (i, j)),
        out_shape=jax.ShapeDtypeStruct((t, D_MODEL), F32),
        compiler_params=_cparams(("parallel", "parallel")),
        name="out_proj",
    )(merged, w_out, x)


def _router_kernel(h_ref, nw_ref, wr_ref, br_ref, xn_ref, route_ref, count_ref, carry_ref):
    tr = h_ref.shape[0]

    @pl.when(pl.program_id(0) == 0)
    def _():
        carry_ref[...] = jnp.zeros_like(carry_ref)

    h = h_ref[...]
    xn = h * lax.rsqrt(jnp.mean(h * h, axis=-1, keepdims=True) + EPS) * nw_ref[...]
    xn_ref[...] = _pack_bf16_pairs(xn)
    logits = jnp.dot(xn, wr_ref[...], preferred_element_type=F32, precision=lax.Precision.HIGHEST) + br_ref[...]
    lane = lax.broadcasted_iota(jnp.int32, (tr, LANES), 1)
    neg = jnp.float32(-jnp.inf)
    work = jnp.where(lane < N_EXPERTS, logits, neg)
    vals, idxs = [], []
    onehot = jnp.zeros((tr, LANES), F32)
    for _ in range(TOP_K):
        mx = jnp.max(work, axis=-1, keepdims=True)
        ix = jnp.min(jnp.where(work == mx, lane, LANES), axis=-1, keepdims=True)
        sel = lane == ix
        onehot = jnp.where(sel, 1.0, onehot)
        work = jnp.where(sel, neg, work)
        vals.append(mx)
        idxs.append(ix)
    exps = [jnp.exp(v - vals[0]) for v in vals]
    denom = exps[0] + exps[1] + exps[2] + exps[3]
    ri = lax.broadcasted_iota(jnp.int32, (tr, tr), 0)
    rj = lax.broadcasted_iota(jnp.int32, (tr, tr), 1)
    lower = (ri > rj).astype(BF16)
    carry = carry_ref[0:1, :]
    rank = carry + _dot(lower, onehot.astype(BF16))
    out = jnp.zeros((tr, LANES), F32)
    for kk in range(TOP_K):
        rk = jnp.sum(jnp.where(lane == idxs[kk], rank, 0.0), axis=-1, keepdims=True)
        out = jnp.where(lane == kk, idxs[kk].astype(F32), out)
        out = jnp.where(lane == TOP_K + kk, exps[kk] / denom, out)
        out = jnp.where(lane == 2 * TOP_K + kk, rk, out)
    route_ref[...] = out
    new_carry = carry + jnp.sum(onehot, axis=0, keepdims=True)
    carry_ref[...] = jnp.broadcast_to(new_carry, carry_ref.shape)
    count_ref[...] = jnp.broadcast_to(new_carry, count_ref.shape)


def _router(h, nw, w_router_pad, b_router_pad, tr):
    t, d = h.shape
    return pl.pallas_call(
        _router_kernel,
        grid=(t // tr,),
        in_specs=[
            pl.BlockSpec((tr, d), lambda i: (i, 0)),
            pl.BlockSpec((1, d), lambda i: (0, 0)),
            pl.BlockSpec((d, LANES), lambda i: (0, 0)),
            pl.BlockSpec((1, LANES), lambda i: (0, 0)),
        ],
        out_specs=[
            pl.BlockSpec((tr, d // 2), lambda i: (i, 0)),
            pl.BlockSpec((tr, LANES), lambda i: (i, 0)),
            pl.BlockSpec((8, LANES), lambda i: (0, 0)),
        ],
        out_shape=[
            jax.ShapeDtypeStruct((t, d // 2), jnp.uint32),
            jax.ShapeDtypeStruct((t, LANES), F32),
            jax.ShapeDtypeStruct((8, LANES), F32),
        ],
        scratch_shapes=[pltpu.VMEM((8, LANES), F32)],
        compiler_params=_cparams(("arbitrary",)),
        name="moe_router",
    )(h, nw, w_router_pad, b_router_pad)


def _row_copy(src_hbm, dst_ref, sem, src_row, dst_row):
    return pltpu.make_async_copy(src_hbm.at[pl.ds(src_row, 1), :], dst_ref.at[pl.ds(dst_row, 1), :], sem)


def _dispatch_kernel(idx_ref, src_hbm, o_ref, sem):
    tr = o_ref.shape[0]

    def start(r, carry):
        _row_copy(src_hbm, o_ref, sem, idx_ref[r], r).start()
        return carry

    lax.fori_loop(0, tr, start, 0, unroll=8)

    def wait(r, carry):
        _row_copy(src_hbm, o_ref, sem, 0, r).wait()
        return carry

    lax.fori_loop(0, tr, wait, 0, unroll=8)


def _dispatch(buf_tok, src, tr):
    n_rows = buf_tok.shape[0]
    d = src.shape[1]
    return pl.pallas_call(
        _dispatch_kernel,
        grid=(n_rows // tr,),
        in_specs=[
            pl.BlockSpec((tr,), lambda i: (i,), memory_space=pltpu.SMEM),
            pl.BlockSpec(memory_space=pl.ANY),
        ],
        out_specs=pl.BlockSpec((tr, d), lambda i: (i, 0)),
        out_shape=jax.ShapeDtypeStruct((n_rows, d), src.dtype),
        scratch_shapes=[pltpu.SemaphoreType.DMA(())],
        compiler_params=_cparams(("arbitrary",)),
        name="moe_dispatch",
    )(buf_tok, src)


def _block_copy(src_ref, dst_hbm, sem, src_blk, dst_blk):
    return pltpu.make_async_copy(src_ref.at[pl.ds(src_blk * MOE_BLOCK, MOE_BLOCK), :],
                                 dst_hbm.at[pl.ds(pl.multiple_of(dst_blk * MOE_BLOCK, MOE_BLOCK), MOE_BLOCK), :],
                                 sem)


def _expert_kernel(ce_ref, cs_ref, cn_ref, cw_ref, x_ref, wg_ref, bg_ref, wu_ref, bu_ref, wd_ref, bd_ref,
                   y_hbm, xb_ref, acc_ref, ybuf_ref, sem, *, n_blocks):
    i = pl.program_id(0)
    j = pl.program_id(1)
    nf = pl.num_programs(1)
    c = cn_ref[i]
    start = cs_ref[i]
    win_off = (start - cw_ref[i]) * MOE_BLOCK
    last = j == nf - 1

    def run(row0, m):
        rows = pl.ds(row0, m)

        @pl.when(j == 0)
        def _():
            lo, hi = _unpack_bf16_pairs(x_ref[pl.ds(pl.multiple_of(win_off + row0, MOE_BLOCK), m), :])
            xb_ref[rows, :] = jnp.concatenate([lo.astype(BF16), hi.astype(BF16)], axis=1)

        xb = xb_ref[rows, :]
        gate = _dot(xb, wg_ref[...].astype(BF16)) + bg_ref[...]
        up = _dot(xb, wu_ref[...].astype(BF16)) + bu_ref[...]
        gate = jnp.minimum(gate, SWIGLU_LIMIT)
        up = jnp.clip(up, -SWIGLU_LIMIT, SWIGLU_LIMIT)
        hid = (up + 1.0) * gate * _sigmoid(SWIGLU_ALPHA * gate)
        part = _dot(hid.astype(BF16), wd_ref[...].astype(BF16))

        @pl.when(j == 0)
        def _():
            acc_ref[rows, :] = part + bd_ref[...]

        @pl.when((j > 0) & jnp.logical_not(last))
        def _():
            acc_ref[rows, :] += part

        @pl.when(last)
        def _():
            ybuf_ref[rows, :] = _pack_bf16_pairs(acc_ref[rows, :] + part)

    @pl.when(c == 4)
    def _():
        run(0, 4 * MOE_BLOCK)

    @pl.when((c == 2) | (c == 3))
    def _():
        run(0, 2 * MOE_BLOCK)

    @pl.when(c == 3)
    def _():
        run(2 * MOE_BLOCK, MOE_BLOCK)

    @pl.when(c == 1)
    def _():
        run(0, MOE_BLOCK)

    @pl.when(last & (c == 0))
    def _():
        ybuf_ref[...] = jnp.zeros_like(ybuf_ref)

    @pl.when(last)
    def _():
        def writes(r):
            return (r < c) | ((c == 0) & (start + r < n_blocks))

        for r in range(MOE_CHUNK):
            @pl.when(writes(r))
            def _():
                _block_copy(ybuf_ref, y_hbm, sem, r, start + r).start()

        for r in range(MOE_CHUNK):
            @pl.when(writes(r))
            def _():
                _block_copy(ybuf_ref, y_hbm, sem, r, start + r).wait()


def _experts(c_expert, c_start, c_count, c_window, x_sorted, w_gate, b_gate, w_up, b_up, w_down, b_down, tf):
    n_rows, dp = x_sorted.shape
    d = 2 * dp
    nf = D_FF // tf
    n_chunks = c_expert.shape[0]
    rows = MOE_CHUNK * MOE_BLOCK

    def tile(i, j, cn):
        return jnp.where(cn[i] > 0, j, nf - 1)

    grid_spec = pltpu.PrefetchScalarGridSpec(
        num_scalar_prefetch=4,
        grid=(n_chunks, nf),
        in_specs=[
            pl.BlockSpec((pl.Element(rows), pl.Element(dp)),
                         lambda i, j, ce, cs, cn, cw: (cw[i] * MOE_BLOCK, 0)),
            pl.BlockSpec((None, d, tf), lambda i, j, ce, cs, cn, cw: (ce[i], 0, tile(i, j, cn))),
            pl.BlockSpec((None, 1, tf), lambda i, j, ce, cs, cn, cw: (ce[i], 0, tile(i, j, cn))),
            pl.BlockSpec((None, d, tf), lambda i, j, ce, cs, cn, cw: (ce[i], 0, tile(i, j, cn))),
            pl.BlockSpec((None, 1, tf), lambda i, j, ce, cs, cn, cw: (ce[i], 0, tile(i, j, cn))),
            pl.BlockSpec((None, tf, d), lambda i, j, ce, cs, cn, cw: (ce[i], tile(i, j, cn), 0)),
            pl.BlockSpec((None, 1, d), lambda i, j, ce, cs, cn, cw: (ce[i], 0, 0)),
        ],
        out_specs=pl.BlockSpec(memory_space=pl.ANY),
        scratch_shapes=[pltpu.VMEM((rows, d), BF16), pltpu.VMEM((rows, d), F32),
                        pltpu.VMEM((rows, dp), jnp.uint32), pltpu.SemaphoreType.DMA(())],
    )
    return pl.pallas_call(
        functools.partial(_expert_kernel, n_blocks=n_rows // MOE_BLOCK),
        grid_spec=grid_spec,
        out_shape=jax.ShapeDtypeStruct((n_rows, dp), jnp.uint32),
        compiler_params=pltpu.CompilerParams(dimension_semantics=("arbitrary", "arbitrary"),
                                             vmem_limit_bytes=EXPERT_VMEM_LIMIT),
        name="moe_experts",
    )(c_expert, c_start, c_count, c_window, x_sorted, w_gate, b_gate, w_up, b_up, w_down, b_down)


def _combine_kernel(dest_ref, y_hbm, h_ref, route_ref, nw_ref, o_ref, buf_ref, sem):
    tr = h_ref.shape[0]

    def start(r, carry):
        for kk in range(TOP_K):
            _row_copy(y_hbm, buf_ref.at[kk], sem, dest_ref[r * TOP_K + kk], r).start()
        return carry

    lax.fori_loop(0, tr, start, 0, unroll=2)

    def wait(r, carry):
        for kk in range(TOP_K):
            _row_copy(y_hbm, buf_ref.at[kk], sem, 0, r).wait()
        return carry

    lax.fori_loop(0, tr, wait, 0, unroll=2)

    route = route_ref[...]
    h = h_ref[...]
    dp = h.shape[1] // 2
    acc_lo = h[:, :dp]
    acc_hi = h[:, dp:]
    for kk in range(TOP_K):
        w = route[:, TOP_K + kk:TOP_K + kk + 1]
        lo, hi = _unpack_bf16_pairs(buf_ref[kk])
        acc_lo = acc_lo + w * lo
        acc_hi = acc_hi + w * hi
    acc = jnp.concatenate([acc_lo, acc_hi], axis=1)
    o_ref[...] = acc * lax.rsqrt(jnp.mean(acc * acc, axis=-1, keepdims=True) + EPS) * nw_ref[...]


def _combine(dest, y_sorted, h, route, nw, tr):
    t, d = h.shape
    return pl.pallas_call(
        _combine_kernel,
        grid=(t // tr,),
        in_specs=[
            pl.BlockSpec((tr * TOP_K,), lambda i: (i,), memory_space=pltpu.SMEM),
            pl.BlockSpec(memory_space=pl.ANY),
            pl.BlockSpec((tr, d), lambda i: (i, 0)),
            pl.BlockSpec((tr, LANES), lambda i: (i, 0)),
            pl.BlockSpec((1, d), lambda i: (0, 0)),
        ],
        out_specs=pl.BlockSpec((tr, d), lambda i: (i, 0)),
        out_shape=jax.ShapeDtypeStruct((t, d), F32),
        scratch_shapes=[pltpu.VMEM((TOP_K, tr, d // 2), jnp.uint32), pltpu.SemaphoreType.DMA(())],
        compiler_params=_cparams(("arbitrary",)),
        name="moe_combine",
    )(dest, y_sorted, h, route, nw)


def _deinterleave_perm():
    half = RET_QK_DIM // 2
    per_head = jnp.concatenate([jnp.arange(half) * 2, jnp.arange(half) * 2 + 1])
    return (jnp.arange(RET_HEADS)[:, None] * RET_QK_DIM + per_head[None, :]).reshape(-1)


def _split_w_in(w_in):
    o = 0
    parts = {}
    for name, width in (("rq", RET_QK), ("rk", RET_QK), ("rv", RET_V), ("rg", RET_V), ("dqkv", DN_CONV_CH),
                        ("dz", DN_V), ("small", 4 * DN_V_HEADS), ("gr", D_MODEL), ("gd", D_MODEL)):
        parts[name] = w_in[:, o:o + width]
        o += width
    perm = _deinterleave_perm()
    w_main = jnp.concatenate(
        [parts["rq"][:, perm], parts["rk"][:, perm], parts["rv"], parts["rg"], parts["dqkv"], parts["dz"],
         parts["gr"], parts["gd"]], axis=1).astype(BF16)
    w_small = jnp.pad(parts["small"], ((0, 0), (0, LANES - 4 * DN_V_HEADS))).astype(BF16)
    return w_main, w_small


def _gate_rows(gates, batch, seq):
    nh = DN_V_HEADS
    rows = gates[:, 2 * nh:4 * nh].reshape(batch, seq, 2, DN_K_HEADS, 2)
    rows = rows.transpose(0, 3, 2, 4, 1).reshape(batch, DN_K_HEADS, 4, seq)
    return jnp.concatenate([rows, jnp.zeros_like(rows)], axis=2)


def _token_mixer(x2, batch, seq, norm1_w, w_in, conv_w, a_log_f, a_log_b, dt_f, dt_b, dn_norm_w,
                 w_br, w_bd, w_out):
    t = batch * seq
    w_main, w_small = _split_w_in(w_in)
    nw1 = norm1_w.reshape(1, D_MODEL)
    proj = _norm_matmul(x2, nw1, w_main, BF16, tm=min(2048, t), tn=512)

    zeros = jnp.zeros((2 * DN_V_HEADS,), F32)
    pad = jnp.zeros((LANES - 4 * DN_V_HEADS,), F32)
    alog_row = jnp.concatenate([zeros, a_log_f, a_log_b, pad]).reshape(1, LANES)
    dtb_row = jnp.concatenate([zeros, dt_f, dt_b, pad]).reshape(1, LANES)
    gates = _gate_proj(x2, nw1, w_small, alog_row, dtb_row, tm=min(512, t))

    angle = 1.0 / (ROPE_BASE ** jnp.linspace(0.0, 1.0, RET_QK_DIM // 2, dtype=F32))
    theta = jnp.arange(seq, dtype=F32)[:, None] * angle[None, :]
    log_g = jnp.log(1.0 - 2.0 ** (-5.0 - jnp.arange(2 * RET_HEADS, dtype=F32)))
    ret = _retention(proj, jnp.cos(theta), jnp.sin(theta), log_g, batch, seq)

    dn_qkv = _dn_prep(proj, conv_w, seq, tr=min(512, seq), tc=512)
    o_f, o_b = _dn_scan(dn_qkv, gates, _gate_rows(gates, batch, seq), batch, seq)
    dn = _dn_final(o_f, o_b, proj, dn_norm_w.reshape(1, DN_DIM), tr=min(512, t), tc=512)

    merged = _merge(ret, dn, w_br.astype(BF16), w_bd.astype(BF16), proj, tm=min(1024, t), tn=512)
    return _out_proj(merged, w_out.astype(BF16), x2, tm=min(1024, t), tn=512)


def _moe(h, norm2_w, w_router, b_router, w_gate, b_gate, w_up, b_up, w_down, b_down, norm_f_w):
    t = h.shape[0]
    wr = jnp.pad(w_router, ((0, 0), (0, LANES - N_EXPERTS)))
    br = jnp.pad(b_router, (0, LANES - N_EXPERTS)).reshape(1, LANES)
    xn, route, counts = _router(h, norm2_w.reshape(1, D_MODEL), wr, br, tr=min(256, t))

    top_idx = route[:, :TOP_K].astype(jnp.int32)
    rank = route[:, 2 * TOP_K:3 * TOP_K].astype(jnp.int32)
    counts = counts[0, :N_EXPERTS].astype(jnp.int32)
    n_assign = t * TOP_K
    n_rows = ((n_assign + N_EXPERTS * (MOE_BLOCK - 1) + MOE_BLOCK - 1) // MOE_BLOCK) * MOE_BLOCK
    n_blocks = n_rows // MOE_BLOCK
    padded = ((counts + MOE_BLOCK - 1) // MOE_BLOCK) * MOE_BLOCK
    pad_ends = jnp.cumsum(padded)
    pad_starts = pad_ends - padded
    dest = (pad_starts[top_idx] + rank).reshape(-1)
    tok = jnp.repeat(jnp.arange(t, dtype=jnp.int32), TOP_K)
    buf_tok = jnp.zeros((n_rows,), jnp.int32).at[dest].set(tok)
    nb_e = padded // MOE_BLOCK
    blk_start_e = pad_starts // MOE_BLOCK
    nc_e = (nb_e + MOE_CHUNK - 1) // MOE_CHUNK
    c_end_e = jnp.cumsum(nc_e)
    n_chunks = N_EXPERTS + n_blocks // MOE_CHUNK
    cid = jnp.arange(n_chunks, dtype=jnp.int32)
    n_live = c_end_e[-1]
    cid_c = jnp.minimum(cid, n_live - 1)
    e_of = jnp.sum((cid_c[:, None] >= c_end_e[None, :]).astype(jnp.int32), axis=1)
    local = cid_c - (c_end_e - nc_e)[e_of]
    c_start = blk_start_e[e_of] + local * MOE_CHUNK
    c_count = jnp.clip(nb_e[e_of] - local * MOE_CHUNK, 0, MOE_CHUNK)
    c_window = jnp.minimum(c_start, n_blocks - MOE_CHUNK)
    idle = cid >= n_live
    n_used = pad_ends[-1] // MOE_BLOCK
    c_start = jnp.where(idle, n_used + (cid - n_live) * MOE_CHUNK, c_start).astype(jnp.int32)
    c_count = jnp.where(idle, 0, c_count).astype(jnp.int32)

    x_sorted = _dispatch(buf_tok, xn, tr=8 * MOE_BLOCK)
    y_sorted = _experts(e_of.astype(jnp.int32), c_start, c_count, c_window.astype(jnp.int32), x_sorted,
                        w_gate, b_gate.reshape(N_EXPERTS, 1, D_FF),
                        w_up, b_up.reshape(N_EXPERTS, 1, D_FF),
                        w_down, b_down.reshape(N_EXPERTS, 1, D_MODEL), tf=256)
    return _combine(dest, y_sorted, h, route, norm_f_w.reshape(1, D_MODEL), tr=min(512, t))


def kernel(x, norm1_w, w_in, conv_w, dn_a_log_f, dn_a_log_b, dn_dt_bias_f, dn_dt_bias_b, dn_norm_w, w_branch_ret, w_branch_dn, w_out, norm2_w, w_router, b_router, w_gate, b_gate, w_up, b_up, w_down, b_down, norm_f_w):
    batch, seq, d = x.shape
    assert norm1_w.shape[0] == 1, "one layer"
    x2 = x.reshape(batch * seq, d)
    h = _token_mixer(x2, batch, seq, norm1_w[0], w_in[0], conv_w[0], dn_a_log_f[0], dn_a_log_b[0],
                     dn_dt_bias_f[0], dn_dt_bias_b[0], dn_norm_w[0], w_branch_ret[0], w_branch_dn[0], w_out[0])
    out = _moe(h, norm2_w[0], w_router[0], b_router[0], w_gate[0], b_gate[0], w_up[0], b_up[0],
               w_down[0], b_down[0], norm_f_w)
    return out.reshape(batch, seq, d)
```

```python
import functools

import jax
import jax.numpy as jnp
from jax import lax
from jax.experimental import pallas as pl
from jax.experimental.pallas import tpu as pltpu

F32 = jnp.float32
BF16 = jnp.bfloat16

D_MODEL = 2048
RET_HEADS = 4
RET_QK_DIM = 256
RET_V_DIM = 512
ROPE_BASE = 10000.0
DN_K_HEADS = 8
DN_V_HEADS = 16
DN_DIM = 128
DN_CONV = 4
DN_CHUNK = 128
N_EXPERTS = 32
TOP_K = 4
D_FF = 2048
SWIGLU_LIMIT = 7.0
SWIGLU_ALPHA = 1.702
EPS = 1e-6

RET_QK = RET_HEADS * RET_QK_DIM
RET_V = RET_HEADS * RET_V_DIM
DN_QK = DN_K_HEADS * DN_DIM
DN_V = DN_V_HEADS * DN_DIM
DN_CONV_CH = 2 * DN_QK + DN_V

OFF_RQ = 0
OFF_RK = OFF_RQ + RET_QK
OFF_RV = OFF_RK + RET_QK
OFF_RG = OFF_RV + RET_V
OFF_DQKV = OFF_RG + RET_V
OFF_DZ = OFF_DQKV + DN_CONV_CH
OFF_GR = OFF_DZ + DN_V
OFF_GD = OFF_GR + D_MODEL
N_MAIN = OFF_GD + D_MODEL
LANES = 128

RET_CHUNK = 256
DN_TILE = 256
DN_GROUPS_PER_STEP = 2
MOE_BLOCK = 256
MOE_CHUNK = 4
VMEM_LIMIT = 56 * 1024 * 1024
EXPERT_VMEM_LIMIT = 60 * 1024 * 1024


def _cparams(sem):
    return pltpu.CompilerParams(dimension_semantics=sem, vmem_limit_bytes=VMEM_LIMIT)


def _dot(a, b):
    return jnp.dot(a, b, preferred_element_type=F32)


def _dot_nt(a, b):
    return lax.dot_general(a, b, (((1,), (1,)), ((), ())), preferred_element_type=F32)


def _dot_tn(a, b):
    return lax.dot_general(a, b, (((0,), (0,)), ((), ())), preferred_element_type=F32)


def _sigmoid(x):
    return 1.0 / (1.0 + jnp.exp(-x))


def _pack_bf16_pairs(x):
    n = x.shape[1] // 2
    bits = lax.bitcast_convert_type(x.astype(BF16).astype(F32), jnp.uint32)
    return (bits[:, n:] & jnp.uint32(0xFFFF0000)) | (bits[:, :n] >> 16)


def _unpack_bf16_pairs(p):
    lo = lax.bitcast_convert_type(p << 16, F32)
    hi = lax.bitcast_convert_type(p & jnp.uint32(0xFFFF0000), F32)
    return lo, hi


def _norm_matmul_kernel(x_ref, nw_ref, w_ref, o_ref, xn_ref):
    @pl.when(pl.program_id(1) == 0)
    def _():
        x = x_ref[...]
        ms = jnp.mean(x * x, axis=-1, keepdims=True)
        xn_ref[...] = (x * lax.rsqrt(ms + EPS) * nw_ref[...]).astype(BF16)

    o_ref[...] = _dot(xn_ref[...], w_ref[...]).astype(o_ref.dtype)


def _norm_matmul(x, nw, w, out_dtype, tm, tn):
    t, d = x.shape
    n = w.shape[1]
    return pl.pallas_call(
        _norm_matmul_kernel,
        grid=(t // tm, n // tn),
        in_specs=[
            pl.BlockSpec((tm, d), lambda i, j: (i, 0)),
            pl.BlockSpec((1, d), lambda i, j: (0, 0)),
            pl.BlockSpec((d, tn), lambda i, j: (0, j)),
        ],
        out_specs=pl.BlockSpec((tm, tn), lambda i, j: (i, j)),
        out_shape=jax.ShapeDtypeStruct((t, n), out_dtype),
        scratch_shapes=[pltpu.VMEM((tm, d), BF16)],
        compiler_params=_cparams(("parallel", "arbitrary")),
        name="norm_in_proj",
    )(x, nw, w)


def _gate_proj_kernel(x_ref, nw_ref, w_ref, alog_ref, dtb_ref, o_ref):
    x = x_ref[...]
    ms = jnp.mean(x * x, axis=-1, keepdims=True)
    xn = (x * lax.rsqrt(ms + EPS) * nw_ref[...]).astype(BF16)
    p = _dot(xn, w_ref[...])
    lane = lax.broadcasted_iota(jnp.int32, p.shape, 1)
    beta = _sigmoid(p)
    z = p + dtb_ref[...]
    softplus = jnp.maximum(z, 0.0) + jnp.log(1.0 + jnp.exp(-jnp.abs(z)))
    g = -jnp.exp(alog_ref[...]) * softplus
    tm = p.shape[0]
    ci = lax.broadcasted_iota(jnp.int32, (tm, tm), 0)
    cj = lax.broadcasted_iota(jnp.int32, (tm, tm), 1)
    same = _blk(ci, DN_CHUNK) == _blk(cj, DN_CHUNK)

    def chunk_sum(mask):
        return jnp.dot(mask.astype(F32), g, preferred_element_type=F32, precision=lax.Precision.HIGHEST)

    csum_f = chunk_sum(same & (ci >= cj))
    csum_b = chunk_sum(same & (ci <= cj))
    ctot = pltpu.roll(chunk_sum(same), 2 * DN_V_HEADS, 1)
    nh = DN_V_HEADS
    o_ref[...] = jnp.where(lane < 2 * nh, beta,
                           jnp.where(lane < 3 * nh, csum_f,
                                     jnp.where(lane < 4 * nh, csum_b,
                                               jnp.where(lane < 6 * nh, ctot, 0.0))))


def _gate_proj(x, nw, w_small, alog_row, dtb_row, tm):
    t, d = x.shape
    return pl.pallas_call(
        _gate_proj_kernel,
        grid=(t // tm,),
        in_specs=[
            pl.BlockSpec((tm, d), lambda i: (i, 0)),
            pl.BlockSpec((1, d), lambda i: (0, 0)),
            pl.BlockSpec((d, LANES), lambda i: (0, 0)),
            pl.BlockSpec((1, LANES), lambda i: (0, 0)),
            pl.BlockSpec((1, LANES), lambda i: (0, 0)),
        ],
        out_specs=pl.BlockSpec((tm, LANES), lambda i: (i, 0)),
        out_shape=jax.ShapeDtypeStruct((t, LANES), F32),
        compiler_params=_cparams(("parallel",)),
        name="dn_gate_proj",
    )(x, nw, w_small, alog_row, dtb_row)


def _rotary(t, cos, sin):
    t = t.astype(F32)
    half = RET_QK_DIM // 2
    t1 = t[:, :half]
    t2 = t[:, half:]
    return jnp.concatenate([t1 * cos - t2 * sin, t2 * cos + t1 * sin], axis=1)


def _ret_heads(q_ref, k_ref, v_ref, cos_ref, sin_ref):
    cos = cos_ref[...]
    sin = sin_ref[...]
    dk, dv = RET_QK_DIM, RET_V_DIM
    qs = [_rotary(q_ref[:, h * dk:(h + 1) * dk], cos, sin) for h in range(RET_HEADS)]
    ks = [_rotary(k_ref[:, h * dk:(h + 1) * dk], cos, sin) * (dk ** -0.5) for h in range(RET_HEADS)]
    vs = [v_ref[:, h * dv:(h + 1) * dv] for h in range(RET_HEADS)]
    return qs, ks, vs


def _ret_fwd_kernel(lg_ref, q_ref, k_ref, v_ref, cos_ref, sin_ref, o_ref, state_ref, dmask_ref):
    c = RET_CHUNK
    heads = range(RET_HEADS)
    lg_f = [lg_ref[2 * h] for h in heads]
    lg_b = [lg_ref[2 * h + 1] for h in heads]

    @pl.when(pl.program_id(1) == 0)
    def _():
        state_ref[...] = jnp.zeros_like(state_ref)
        i = lax.broadcasted_iota(jnp.int32, (c, c), 0)
        j = lax.broadcasted_iota(jnp.int32, (c, c), 1)
        d = (i - j).astype(F32)
        for h in heads:
            dmask_ref[h] = jnp.where(d >= 0, jnp.exp(d * lg_f[h]), jnp.exp(-d * lg_b[h]))

    qs, ks, vs = _ret_heads(q_ref, k_ref, v_ref, cos_ref, sin_ref)
    idx = lax.broadcasted_iota(jnp.int32, (c, 1), 0).astype(F32)
    scores = [_dot_nt(qs[h].astype(BF16), ks[h].astype(BF16)) * dmask_ref[h] for h in heads]
    states = [state_ref[h] for h in heads]
    inter = [_dot((qs[h] * jnp.exp((idx + 1.0) * lg_f[h])).astype(BF16), states[h].astype(BF16)) for h in heads]
    outs = [_dot(scores[h].astype(BF16), vs[h]) + inter[h] for h in heads]
    for h in heads:
        c_dec = jnp.exp(jnp.full((1, 1), c, F32) * lg_f[h])
        k_dec = jnp.exp((c - 1.0 - idx) * lg_f[h])
        state_ref[h] = states[h] * c_dec + _dot_tn((ks[h] * k_dec).astype(BF16), vs[h])
    o_ref[...] = jnp.concatenate(outs, axis=1)


def _ret_bwd_kernel(lg_ref, q_ref, k_ref, v_ref, cos_ref, sin_ref, oacc_ref, rg_ref, o_ref, state_ref):
    c = RET_CHUNK
    heads = range(RET_HEADS)
    dv = RET_V_DIM
    lg_b = [lg_ref[2 * h + 1] for h in heads]

    @pl.when(pl.program_id(1) == 0)
    def _():
        state_ref[...] = jnp.zeros_like(state_ref)

    qs, ks, vs = _ret_heads(q_ref, k_ref, v_ref, cos_ref, sin_ref)
    idx = lax.broadcasted_iota(jnp.int32, (c, 1), 0).astype(F32)
    states = [state_ref[h] for h in heads]
    inter = [_dot((qs[h] * jnp.exp((c - idx) * lg_b[h])).astype(BF16), states[h].astype(BF16)) for h in heads]
    for h in heads:
        c_dec = jnp.exp(jnp.full((1, 1), c, F32) * lg_b[h])
        k_dec = jnp.exp(idx * lg_b[h])
        state_ref[h] = states[h] * c_dec + _dot_tn((ks[h] * k_dec).astype(BF16), vs[h])
    outs = []
    for h in heads:
        o = oacc_ref[:, h * dv:(h + 1) * dv] + inter[h]
        o = o * lax.rsqrt(jnp.mean(o * o, axis=-1, keepdims=True) + EPS)
        rg = rg_ref[:, h * dv:(h + 1) * dv].astype(F32)
        outs.append((o * (rg * _sigmoid(rg))).astype(o_ref.dtype))
    o_ref[...] = jnp.concatenate(outs, axis=1)


def _retention(proj, cos, sin, log_g, batch, seq):
    c = RET_CHUNK
    nc = seq // c
    t = batch * seq
    half = RET_QK_DIM // 2
    smem = pl.BlockSpec(memory_space=pltpu.SMEM)
    state = pltpu.VMEM((RET_HEADS, RET_QK_DIM, RET_V_DIM), F32)

    def row_f(b, n):
        return b * nc + n

    def row_b(b, n):
        return b * nc + (nc - 1 - n)

    def specs(row):
        return [
            pl.BlockSpec((c, RET_QK), lambda b, n: (row(b, n), OFF_RQ // RET_QK)),
            pl.BlockSpec((c, RET_QK), lambda b, n: (row(b, n), OFF_RK // RET_QK)),
            pl.BlockSpec((c, RET_V), lambda b, n: (row(b, n), OFF_RV // RET_V)),
        ]

    o_acc = pl.pallas_call(
        _ret_fwd_kernel,
        grid=(batch, nc),
        in_specs=[smem] + specs(row_f) + [
            pl.BlockSpec((c, half), lambda b, n: (n, 0)),
            pl.BlockSpec((c, half), lambda b, n: (n, 0)),
        ],
        out_specs=pl.BlockSpec((c, RET_V), lambda b, n: (row_f(b, n), 0)),
        out_shape=jax.ShapeDtypeStruct((t, RET_V), F32),
        scratch_shapes=[state, pltpu.VMEM((RET_HEADS, c, c), F32)],
        compiler_params=_cparams(("parallel", "arbitrary")),
        name="retention_fwd",
    )(log_g, proj, proj, proj, cos, sin)

    return pl.pallas_call(
        _ret_bwd_kernel,
        grid=(batch, nc),
        in_specs=[smem] + specs(row_b) + [
            pl.BlockSpec((c, half), lambda b, n: (nc - 1 - n, 0)),
            pl.BlockSpec((c, half), lambda b, n: (nc - 1 - n, 0)),
            pl.BlockSpec((c, RET_V), lambda b, n: (row_b(b, n), 0)),
            pl.BlockSpec((c, RET_V), lambda b, n: (row_b(b, n), OFF_RG // RET_V)),
        ],
        out_specs=pl.BlockSpec((c, RET_V), lambda b, n: (row_b(b, n), 0)),
        out_shape=jax.ShapeDtypeStruct((t, RET_V), BF16),
        scratch_shapes=[state],
        compiler_params=_cparams(("parallel", "arbitrary")),
        name="retention_bwd",
    )(log_g, proj, proj, proj, cos, sin, o_acc, proj)


DN_HALO = 16


def _dn_prep_kernel(x_ref, prev_ref, next_ref, w_ref, o_ref, *, tiles_per_seq):
    tr = x_ref.shape[0]
    i = pl.program_id(0)
    j = pl.program_id(1)
    pos = i % tiles_per_seq
    has_prev = (pos != 0).astype(F32)
    has_next = (pos != tiles_per_seq - 1).astype(F32)
    x = x_ref[...].astype(F32)
    xc = jnp.concatenate(
        [prev_ref[...].astype(F32) * has_prev, x, next_ref[...].astype(F32) * has_next], axis=0)
    n = tr + 2 * DN_HALO
    w = w_ref[...]
    lo, hi = DN_HALO, DN_HALO + tr
    y = (pltpu.roll(xc, 2, 0)[lo:hi] * w[0:1, :] + pltpu.roll(xc, 1, 0)[lo:hi] * w[1:2, :]
         + x * w[2:3, :] + pltpu.roll(xc, n - 1, 0)[lo:hi] * w[3:4, :])
    y = y * _sigmoid(y)
    tc = y.shape[1]
    qk_tiles = (2 * DN_QK) // tc
    q_tiles = DN_QK // tc
    scale = jnp.where(j < q_tiles, DN_DIM ** -0.5, 1.0).astype(F32)
    is_qk = j < qk_tiles
    outs = []
    for s in range(tc // DN_DIM):
        ys = y[:, s * DN_DIM:(s + 1) * DN_DIM]
        rs = lax.rsqrt(jnp.sum(ys * ys, axis=-1, keepdims=True) + EPS) * scale
        outs.append(ys * jnp.where(is_qk, rs, 1.0))
    o_ref[...] = jnp.concatenate(outs, axis=1).astype(o_ref.dtype)


def _dn_prep(proj, conv_w, seq, tr, tc):
    t = proj.shape[0]
    tiles_per_seq = seq // tr
    nrow = t // tr
    col0 = OFF_DQKV // tc
    hb = tr // DN_HALO
    last_halo = t // DN_HALO - 1
    return pl.pallas_call(
        functools.partial(_dn_prep_kernel, tiles_per_seq=tiles_per_seq),
        grid=(nrow, DN_CONV_CH // tc),
        in_specs=[
            pl.BlockSpec((tr, tc), lambda i, j: (i, col0 + j)),
            pl.BlockSpec((DN_HALO, tc), lambda i, j: (jnp.maximum(i * hb - 1, 0), col0 + j)),
            pl.BlockSpec((DN_HALO, tc), lambda i, j: (jnp.minimum((i + 1) * hb, last_halo), col0 + j)),
            pl.BlockSpec((DN_CONV, tc), lambda i, j: (0, j)),
        ],
        out_specs=pl.BlockSpec((tr, tc), lambda i, j: (i, j)),
        out_shape=jax.ShapeDtypeStruct((t, DN_CONV_CH), BF16),
        compiler_params=_cparams(("parallel", "parallel")),
        name="dn_prep",
    )(proj, proj, proj, conv_w)


def _blk(idx, size):
    return lax.shift_right_logical(idx, size.bit_length() - 1)


def _unit_tri_inverse(ms, ci, cj, chunk):
    eye = (ci == cj).astype(F32)
    base = 8
    diag8 = _blk(ci, base) == _blk(cj, base)
    m8 = [jnp.where(diag8, m, 0.0) for m in ms]
    m8b = [m.astype(BF16) for m in m8]
    p2 = [_dot(m, m) for m in m8b]
    p2b = [p.astype(BF16) for p in p2]
    p4 = [_dot(p, p) for p in p2b]
    inv = [_dot((eye - m).astype(BF16), (eye + p).astype(BF16)) for m, p in zip(m8, p2)]
    inv = [_dot(t.astype(BF16), (eye + p).astype(BF16)) for t, p in zip(inv, p4)]
    s = base
    while s < chunk:
        offmask = (_blk(ci, 2 * s) == _blk(cj, 2 * s)) & (_blk(ci, s) != _blk(cj, s))
        invb = [t.astype(BF16) for t in inv]
        left = [_dot(t, jnp.where(offmask, m, 0.0).astype(BF16)).astype(BF16) for t, m in zip(invb, ms)]
        inv = [t - _dot(lf, tb) for t, lf, tb in zip(inv, left, invb)]
        s *= 2
    return inv


def _dn_scan_kernel(qf_ref, kf_ref, vf_ref, gcf_ref, grf_ref,
                    qb_ref, kb_ref, vb_ref, gcb_ref, grb_ref,
                    of_ref, ob_ref, state_ref):
    r = DN_TILE
    c = DN_CHUNK
    nchunk = r // c
    grp = pl.program_id(1)

    @pl.when(pl.program_id(2) == 0)
    def _():
        state_ref[...] = jnp.zeros_like(state_ref)

    ci = lax.broadcasted_iota(jnp.int32, (r, r), 0)
    cj = lax.broadcasted_iota(jnp.int32, (r, r), 1)
    same = _blk(ci, c) == _blk(cj, c)
    lane = lax.broadcasted_iota(jnp.int32, (r, LANES), 1)
    nh = DN_V_HEADS

    def column(gates, lane_idx):
        return jnp.sum(jnp.where(lane == lane_idx, gates, 0.0), axis=-1, keepdims=True)

    chains = []
    for d, (q_ref, k_ref, v_ref, gc_ref, gr_ref) in enumerate(
            ((qf_ref, kf_ref, vf_ref, gcf_ref, grf_ref), (qb_ref, kb_ref, vb_ref, gcb_ref, grb_ref))):
        rev = d == 1
        before_eq = same & ((ci <= cj) if rev else (ci >= cj))
        strict = same & ((ci < cj) if rev else (ci > cj))
        gates = gc_ref[...]
        for gi in range(DN_GROUPS_PER_STEP):
            q = q_ref[:, gi * DN_DIM:(gi + 1) * DN_DIM]
            k = k_ref[:, gi * DN_DIM:(gi + 1) * DN_DIM]
            kf32 = k.astype(F32)
            qf32 = q.astype(F32)
            kk = _dot_nt(k, k)
            qk = _dot_nt(q, k)
            grow = gr_ref[0, gi]
            for hh in range(2):
                head = 2 * (grp * DN_GROUPS_PER_STEP + gi) + hh
                beta_c = column(gates, d * nh + head)
                gc_c = column(gates, (2 + d) * nh + head)
                gt_c = column(gates, (4 + d) * nh + head)
                gc_r = grow[2 * d + hh:2 * d + hh + 1, :]
                decay = jnp.where(before_eq, jnp.exp(jnp.where(before_eq, gc_c - gc_r, 0.0)), 0.0)
                egc = jnp.exp(gc_c)
                vcol = (2 * gi + hh) * DN_DIM
                v = v_ref[:, vcol:vcol + DN_DIM].astype(F32)
                chains.append(dict(
                    rev=rev, sidx=4 * gi + 2 * d + hh,
                    m=jnp.where(strict, beta_c * kk * decay, 0.0),
                    rhs=jnp.concatenate([v * beta_c, kf32 * (beta_c * egc)], axis=1).astype(BF16),
                    qe=(qf32 * egc).astype(BF16),
                    ke=(kf32 * jnp.exp(gt_c - gc_c)).astype(BF16),
                    qkm=(qk * decay).astype(BF16),
                    egt=jnp.exp(gt_c)))
    tinvs = _unit_tri_inverse([ch["m"] for ch in chains], ci, cj, c)
    for ch, tinv in zip(chains, tinvs):
        uw = _dot(tinv.astype(BF16), ch["rhs"])
        ch["u"] = uw[:, :DN_DIM]
        ch["wb"] = uw[:, DN_DIM:].astype(BF16)
        ch["state"] = state_ref[ch["sidx"]]
        ch["o"] = [None] * nchunk
    for step in range(nchunk):
        for ch in chains:
            cc = nchunk - 1 - step if ch["rev"] else step
            lo, hi = cc * c, (cc + 1) * c
            sb = ch["state"].astype(BF16)
            ws = _dot(jnp.concatenate([ch["wb"][lo:hi], ch["qe"][lo:hi]], axis=0), sb)
            v_new = (ch["u"][lo:hi] - ws[:c]).astype(BF16)
            ch["o"][cc] = ws[c:] + _dot(ch["qkm"][lo:hi, lo:hi], v_new)
            ch["state"] = ch["state"] * ch["egt"][lo:lo + 1, :] + _dot_tn(ch["ke"][lo:hi], v_new)
    for ch in chains:
        state_ref[ch["sidx"]] = ch["state"]
    fwd = sorted((ch for ch in chains if not ch["rev"]), key=lambda ch: ch["sidx"])
    bwd = sorted((ch for ch in chains if ch["rev"]), key=lambda ch: ch["sidx"])
    of_ref[...] = jnp.concatenate([jnp.concatenate(ch["o"], axis=0) for ch in fwd], axis=1).astype(of_ref.dtype)
    ob_ref[...] = jnp.concatenate([jnp.concatenate(ch["o"], axis=0) for ch in bwd], axis=1).astype(ob_ref.dtype)


def _dn_scan(dn_qkv, gates, gates_rows, batch, seq):
    r = DN_TILE
    nt = seq // r
    t = batch * seq
    kcol0 = DN_QK // DN_DIM
    vcol0 = (2 * DN_QK) // (2 * DN_DIM)

    def row_f(b, g, n):
        return b * nt + n

    def row_b(b, g, n):
        return b * nt + (nt - 1 - n)

    ng = DN_GROUPS_PER_STEP

    def specs(row, tile):
        return [
            pl.BlockSpec((r, ng * DN_DIM), lambda b, g, n: (row(b, g, n), g)),
            pl.BlockSpec((r, ng * DN_DIM), lambda b, g, n: (row(b, g, n), kcol0 // ng + g)),
            pl.BlockSpec((r, 2 * ng * DN_DIM), lambda b, g, n: (row(b, g, n), vcol0 // ng + g)),
            pl.BlockSpec((r, LANES), lambda b, g, n: (row(b, g, n), 0)),
            pl.BlockSpec((1, ng, 8, r), lambda b, g, n: (b, g, 0, tile(n))),
        ]

    out_shape = jax.ShapeDtypeStruct((t, DN_V), BF16)
    return pl.pallas_call(
        _dn_scan_kernel,
        grid=(batch, DN_K_HEADS // ng, nt),
        in_specs=specs(row_f, lambda n: n) + specs(row_b, lambda n: nt - 1 - n),
        out_specs=[
            pl.BlockSpec((r, 2 * ng * DN_DIM), lambda b, g, n: (row_f(b, g, n), g)),
            pl.BlockSpec((r, 2 * ng * DN_DIM), lambda b, g, n: (row_b(b, g, n), g)),
        ],
        out_shape=[out_shape, out_shape],
        scratch_shapes=[pltpu.VMEM((4 * ng, DN_DIM, DN_DIM), F32)],
        compiler_params=_cparams(("parallel", "parallel", "arbitrary")),
        name="dn_scan",
    )(dn_qkv, dn_qkv, dn_qkv, gates, gates_rows, dn_qkv, dn_qkv, dn_qkv, gates, gates_rows)


def _dn_final_kernel(of_ref, ob_ref, z_ref, nw_ref, o_ref):
    o = of_ref[...].astype(F32) + ob_ref[...].astype(F32)
    nw = nw_ref[...]
    outs = []
    for s in range(o.shape[1] // DN_DIM):
        os_ = o[:, s * DN_DIM:(s + 1) * DN_DIM]
        outs.append(os_ * lax.rsqrt(jnp.mean(os_ * os_, axis=-1, keepdims=True) + EPS) * nw)
    z = z_ref[...].astype(F32)
    o_ref[...] = (jnp.concatenate(outs, axis=1) * (z * _sigmoid(z))).astype(o_ref.dtype)


def _dn_final(o_f, o_b, proj, norm_w, tr, tc):
    t = o_f.shape[0]
    zcol0 = OFF_DZ // tc
    return pl.pallas_call(
        _dn_final_kernel,
        grid=(t // tr, DN_V // tc),
        in_specs=[
            pl.BlockSpec((tr, tc), lambda i, j: (i, j)),
            pl.BlockSpec((tr, tc), lambda i, j: (i, j)),
            pl.BlockSpec((tr, tc), lambda i, j: (i, zcol0 + j)),
            pl.BlockSpec((1, DN_DIM), lambda i, j: (0, 0)),
        ],
        out_specs=pl.BlockSpec((tr, tc), lambda i, j: (i, j)),
        out_shape=jax.ShapeDtypeStruct((t, DN_V), BF16),
        compiler_params=_cparams(("parallel", "parallel")),
        name="dn_final",
    )(o_f, o_b, proj, norm_w)


def _merge_kernel(ret_ref, dn_ref, wr_ref, wd_ref, gr_ref, gd_ref, o_ref):
    a = _dot(ret_ref[...], wr_ref[...])
    b = _dot(dn_ref[...], wd_ref[...])
    o_ref[...] = (_sigmoid(gr_ref[...].astype(F32)) * a + _sigmoid(gd_ref[...].astype(F32)) * b).astype(o_ref.dtype)


def _merge(ret, dn, w_br, w_bd, proj, tm, tn):
    t = ret.shape[0]
    return pl.pallas_call(
        _merge_kernel,
        grid=(t // tm, D_MODEL // tn),
        in_specs=[
            pl.BlockSpec((tm, RET_V), lambda i, j: (i, 0)),
            pl.BlockSpec((tm, DN_V), lambda i, j: (i, 0)),
            pl.BlockSpec((RET_V, tn), lambda i, j: (0, j)),
            pl.BlockSpec((DN_V, tn), lambda i, j: (0, j)),
            pl.BlockSpec((tm, tn), lambda i, j: (i, OFF_GR // tn + j)),
            pl.BlockSpec((tm, tn), lambda i, j: (i, OFF_GD // tn + j)),
        ],
        out_specs=pl.BlockSpec((tm, tn), lambda i, j: (i, j)),
        out_shape=jax.ShapeDtypeStruct((t, D_MODEL), BF16),
        compiler_params=_cparams(("parallel", "parallel")),
        name="branch_merge",
    )(ret, dn, w_br, w_bd, proj, proj)


def _out_proj_kernel(m_ref, w_ref, x_ref, o_ref):
    o_ref[...] = x_ref[...] + _dot(m_ref[...], w_ref[...])


def _out_proj(merged, w_out, x, tm, tn):
    t = merged.shape[0]
    return pl.pallas_call(
        _out_proj_kernel,
        grid=(t // tm, D_MODEL // tn),
        in_specs=[
            pl.BlockSpec((tm, D_MODEL), lambda i, j: (i, 0)),
            pl.BlockSpec((D_MODEL, tn), lambda i, j: (0, j)),
            pl.BlockSpec((tm, tn), lambda i, j: (i, j)),
        ],
        out_specs=pl.BlockSpec((tm, tn), lambda i, j: (i, j)),
        out_shape=jax.ShapeDtypeStruct((t, D_MODEL), F32),
        compiler_params=_cparams(("parallel", "parallel")),
        name="out_proj",
    )(merged, w_out, x)


def _router_kernel(h_ref, nw_ref, wr_ref, br_ref, xn_ref, route_ref, count_ref, carry_ref):
    tr = h_ref.shape[0]

    @pl.when(pl.program_id(0) == 0)
    def _():
        carry_ref[...] = jnp.zeros_like(carry_ref)

    h = h_ref[...]
    xn = h * lax.rsqrt(jnp.mean(h * h, axis=-1, keepdims=True) + EPS) * nw_ref[...]
    xn_ref[...] = _pack_bf16_pairs(xn)
    logits = jnp.dot(xn, wr_ref[...], preferred_element_type=F32, precision=lax.Precision.HIGHEST) + br_ref[...]
    lane = lax.broadcasted_iota(jnp.int32, (tr, LANES), 1)
    neg = jnp.float32(-jnp.inf)
    work = jnp.where(lane < N_EXPERTS, logits, neg)
    vals, idxs = [], []
    onehot = jnp.zeros((tr, LANES), F32)
    for _ in range(TOP_K):
        mx = jnp.max(work, axis=-1, keepdims=True)
        ix = jnp.min(jnp.where(work == mx, lane, LANES), axis=-1, keepdims=True)
        sel = lane == ix
        onehot = jnp.where(sel, 1.0, onehot)
        work = jnp.where(sel, neg, work)
        vals.append(mx)
        idxs.append(ix)
    exps = [jnp.exp(v - vals[0]) for v in vals]
    denom = exps[0] + exps[1] + exps[2] + exps[3]
    ri = lax.broadcasted_iota(jnp.int32, (tr, tr), 0)
    rj = lax.broadcasted_iota(jnp.int32, (tr, tr), 1)
    lower = (ri > rj).astype(BF16)
    carry = carry_ref[0:1, :]
    rank = carry + _dot(lower, onehot.astype(BF16))
    out = jnp.zeros((tr, LANES), F32)
    for kk in range(TOP_K):
        rk = jnp.sum(jnp.where(lane == idxs[kk], rank, 0.0), axis=-1, keepdims=True)
        out = jnp.where(lane == kk, idxs[kk].astype(F32), out)
        out = jnp.where(lane == TOP_K + kk, exps[kk] / denom, out)
        out = jnp.where(lane == 2 * TOP_K + kk, rk, out)
    route_ref[...] = out
    new_carry = carry + jnp.sum(onehot, axis=0, keepdims=True)
    carry_ref[...] = jnp.broadcast_to(new_carry, carry_ref.shape)
    count_ref[...] = jnp.broadcast_to(new_carry, count_ref.shape)


def _router(h, nw, w_router_pad, b_router_pad, tr):
    t, d = h.shape
    return pl.pallas_call(
        _router_kernel,
        grid=(t // tr,),
        in_specs=[
            pl.BlockSpec((tr, d), lambda i: (i, 0)),
            pl.BlockSpec((1, d), lambda i: (0, 0)),
            pl.BlockSpec((d, LANES), lambda i: (0, 0)),
            pl.BlockSpec((1, LANES), lambda i: (0, 0)),
        ],
        out_specs=[
            pl.BlockSpec((tr, d // 2), lambda i: (i, 0)),
            pl.BlockSpec((tr, LANES), lambda i: (i, 0)),
            pl.BlockSpec((8, LANES), lambda i: (0, 0)),
        ],
        out_shape=[
            jax.ShapeDtypeStruct((t, d // 2), jnp.uint32),
            jax.ShapeDtypeStruct((t, LANES), F32),
            jax.ShapeDtypeStruct((8, LANES), F32),
        ],
        scratch_shapes=[pltpu.VMEM((8, LANES), F32)],
        compiler_params=_cparams(("arbitrary",)),
        name="moe_router",
    )(h, nw, w_router_pad, b_router_pad)


def _row_copy(src_hbm, dst_ref, sem, src_row, dst_row):
    return pltpu.make_async_copy(src_hbm.at[pl.ds(src_row, 1), :], dst_ref.at[pl.ds(dst_row, 1), :], sem)


def _dispatch_kernel(idx_ref, src_hbm, o_ref, sem):
    tr = o_ref.shape[0]

    def start(r, carry):
        _row_copy(src_hbm, o_ref, sem, idx_ref[r], r).start()
        return carry

    lax.fori_loop(0, tr, start, 0, unroll=8)

    def wait(r, carry):
        _row_copy(src_hbm, o_ref, sem, 0, r).wait()
        return carry

    lax.fori_loop(0, tr, wait, 0, unroll=8)


def _dispatch(buf_tok, src, tr):
    n_rows = buf_tok.shape[0]
    d = src.shape[1]
    return pl.pallas_call(
        _dispatch_kernel,
        grid=(n_rows // tr,),
        in_specs=[
            pl.BlockSpec((tr,), lambda i: (i,), memory_space=pltpu.SMEM),
            pl.BlockSpec(memory_space=pl.ANY),
        ],
        out_specs=pl.BlockSpec((tr, d), lambda i: (i, 0)),
        out_shape=jax.ShapeDtypeStruct((n_rows, d), src.dtype),
        scratch_shapes=[pltpu.SemaphoreType.DMA(())],
        compiler_params=_cparams(("arbitrary",)),
        name="moe_dispatch",
    )(buf_tok, src)


def _block_copy(src_ref, dst_hbm, sem, src_blk, dst_blk):
    return pltpu.make_async_copy(src_ref.at[pl.ds(src_blk * MOE_BLOCK, MOE_BLOCK), :],
                                 dst_hbm.at[pl.ds(pl.multiple_of(dst_blk * MOE_BLOCK, MOE_BLOCK), MOE_BLOCK), :],
                                 sem)


def _expert_kernel(ce_ref, cs_ref, cn_ref, cw_ref, x_ref, wg_ref, bg_ref, wu_ref, bu_ref, wd_ref, bd_ref,
                   y_hbm, xb_ref, acc_ref, ybuf_ref, sem, *, n_blocks):
    i = pl.program_id(0)
    j = pl.program_id(1)
    nf = pl.num_programs(1)
    c = cn_ref[i]
    start = cs_ref[i]
    win_off = (start - cw_ref[i]) * MOE_BLOCK
    last = j == nf - 1

    def run(row0, m):
        rows = pl.ds(row0, m)

        @pl.when(j == 0)
        def _():
            lo, hi = _unpack_bf16_pairs(x_ref[pl.ds(pl.multiple_of(win_off + row0, MOE_BLOCK), m), :])
            xb_ref[rows, :] = jnp.concatenate([lo.astype(BF16), hi.astype(BF16)], axis=1)

        xb = xb_ref[rows, :]
        gate = _dot(xb, wg_ref[...].astype(BF16)) + bg_ref[...]
        up = _dot(xb, wu_ref[...].astype(BF16)) + bu_ref[...]
        gate = jnp.minimum(gate, SWIGLU_LIMIT)
        up = jnp.clip(up, -SWIGLU_LIMIT, SWIGLU_LIMIT)
        hid = (up + 1.0) * gate * _sigmoid(SWIGLU_ALPHA * gate)
        part = _dot(hid.astype(BF16), wd_ref[...].astype(BF16))

        @pl.when(j == 0)
        def _():
            acc_ref[rows, :] = part + bd_ref[...]

        @pl.when((j > 0) & jnp.logical_not(last))
        def _():
            acc_ref[rows, :] += part

        @pl.when(last)
        def _():
            ybuf_ref[rows, :] = _pack_bf16_pairs(acc_ref[rows, :] + part)

    @pl.when(c == 4)
    def _():
        run(0, 4 * MOE_BLOCK)

    @pl.when((c == 2) | (c == 3))
    def _():
        run(0, 2 * MOE_BLOCK)

    @pl.when(c == 3)
    def _():
        run(2 * MOE_BLOCK, MOE_BLOCK)

    @pl.when(c == 1)
    def _():
        run(0, MOE_BLOCK)

    @pl.when(last & (c == 0))
    def _():
        ybuf_ref[...] = jnp.zeros_like(ybuf_ref)

    @pl.when(last)
    def _():
        def writes(r):
            return (r < c) | ((c == 0) & (start + r < n_blocks))

        for r in range(MOE_CHUNK):
            @pl.when(writes(r))
            def _():
                _block_copy(ybuf_ref, y_hbm, sem, r, start + r).start()

        for r in range(MOE_CHUNK):
            @pl.when(writes(r))
            def _():
                _block_copy(ybuf_ref, y_hbm, sem, r, start + r).wait()


def _experts(c_expert, c_start, c_count, c_window, x_sorted, w_gate, b_gate, w_up, b_up, w_down, b_down, tf):
    n_rows, dp = x_sorted.shape
    d = 2 * dp
    nf = D_FF // tf
    n_chunks = c_expert.shape[0]
    rows = MOE_CHUNK * MOE_BLOCK

    def tile(i, j, cn):
        return jnp.where(cn[i] > 0, j, nf - 1)

    grid_spec = pltpu.PrefetchScalarGridSpec(
        num_scalar_prefetch=4,
        grid=(n_chunks, nf),
        in_specs=[
            pl.BlockSpec((pl.Element(rows), pl.Element(dp)),
                         lambda i, j, ce, cs, cn, cw: (cw[i] * MOE_BLOCK, 0)),
            pl.BlockSpec((None, d, tf), lambda i, j, ce, cs, cn, cw: (ce[i], 0, tile(i, j, cn))),
            pl.BlockSpec((None, 1, tf), lambda i, j, ce, cs, cn, cw: (ce[i], 0, tile(i, j, cn))),
            pl.BlockSpec((None, d, tf), lambda i, j, ce, cs, cn, cw: (ce[i], 0, tile(i, j, cn))),
            pl.BlockSpec((None, 1, tf), lambda i, j, ce, cs, cn, cw: (ce[i], 0, tile(i, j, cn))),
            pl.BlockSpec((None, tf, d), lambda i, j, ce, cs, cn, cw: (ce[i], tile(i, j, cn), 0)),
            pl.BlockSpec((None, 1, d), lambda i, j, ce, cs, cn, cw: (ce[i], 0, 0)),
        ],
        out_specs=pl.BlockSpec(memory_space=pl.ANY),
        scratch_shapes=[pltpu.VMEM((rows, d), BF16), pltpu.VMEM((rows, d), F32),
                        pltpu.VMEM((rows, dp), jnp.uint32), pltpu.SemaphoreType.DMA(())],
    )
    return pl.pallas_call(
        functools.partial(_expert_kernel, n_blocks=n_rows // MOE_BLOCK),
        grid_spec=grid_spec,
        out_shape=jax.ShapeDtypeStruct((n_rows, dp), jnp.uint32),
        compiler_params=pltpu.CompilerParams(dimension_semantics=("arbitrary", "arbitrary"),
                                             vmem_limit_bytes=EXPERT_VMEM_LIMIT),
        name="moe_experts",
    )(c_expert, c_start, c_count, c_window, x_sorted, w_gate, b_gate, w_up, b_up, w_down, b_down)


def _combine_kernel(dest_ref, y_hbm, h_ref, route_ref, nw_ref, o_ref, buf_ref, sem):
    tr = h_ref.shape[0]

    def start(r, carry):
        for kk in range(TOP_K):
            _row_copy(y_hbm, buf_ref.at[kk], sem, dest_ref[r * TOP_K + kk], r).start()
        return carry

    lax.fori_loop(0, tr, start, 0, unroll=2)

    def wait(r, carry):
        for kk in range(TOP_K):
            _row_copy(y_hbm, buf_ref.at[kk], sem, 0, r).wait()
        return carry

    lax.fori_loop(0, tr, wait, 0, unroll=2)

    route = route_ref[...]
    h = h_ref[...]
    dp = h.shape[1] // 2
    acc_lo = h[:, :dp]
    acc_hi = h[:, dp:]
    for kk in range(TOP_K):
        w = route[:, TOP_K + kk:TOP_K + kk + 1]
        lo, hi = _unpack_bf16_pairs(buf_ref[kk])
        acc_lo = acc_lo + w * lo
        acc_hi = acc_hi + w * hi
    acc = jnp.concatenate([acc_lo, acc_hi], axis=1)
    o_ref[...] = acc * lax.rsqrt(jnp.mean(acc * acc, axis=-1, keepdims=True) + EPS) * nw_ref[...]


def _combine(dest, y_sorted, h, route, nw, tr):
    t, d = h.shape
    return pl.pallas_call(
        _combine_kernel,
        grid=(t // tr,),
        in_specs=[
            pl.BlockSpec((tr * TOP_K,), lambda i: (i,), memory_space=pltpu.SMEM),
            pl.BlockSpec(memory_space=pl.ANY),
            pl.BlockSpec((tr, d), lambda i: (i, 0)),
            pl.BlockSpec((tr, LANES), lambda i: (i, 0)),
            pl.BlockSpec((1, d), lambda i: (0, 0)),
        ],
        out_specs=pl.BlockSpec((tr, d), lambda i: (i, 0)),
        out_shape=jax.ShapeDtypeStruct((t, d), F32),
        scratch_shapes=[pltpu.VMEM((TOP_K, tr, d // 2), jnp.uint32), pltpu.SemaphoreType.DMA(())],
        compiler_params=_cparams(("arbitrary",)),
        name="moe_combine",
    )(dest, y_sorted, h, route, nw)


def _deinterleave_perm():
    half = RET_QK_DIM // 2
    per_head = jnp.concatenate([jnp.arange(half) * 2, jnp.arange(half) * 2 + 1])
    return (jnp.arange(RET_HEADS)[:, None] * RET_QK_DIM + per_head[None, :]).reshape(-1)


def _split_w_in(w_in):
    o = 0
    parts = {}
    for name, width in (("rq", RET_QK), ("rk", RET_QK), ("rv", RET_V), ("rg", RET_V), ("dqkv", DN_CONV_CH),
                        ("dz", DN_V), ("small", 4 * DN_V_HEADS), ("gr", D_MODEL), ("gd", D_MODEL)):
        parts[name] = w_in[:, o:o + width]
        o += width
    perm = _deinterleave_perm()
    w_main = jnp.concatenate(
        [parts["rq"][:, perm], parts["rk"][:, perm], parts["rv"], parts["rg"], parts["dqkv"], parts["dz"],
         parts["gr"], parts["gd"]], axis=1).astype(BF16)
    w_small = jnp.pad(parts["small"], ((0, 0), (0, LANES - 4 * DN_V_HEADS))).astype(BF16)
    return w_main, w_small


def _gate_rows(gates, batch, seq):
    nh = DN_V_HEADS
    rows = gates[:, 2 * nh:4 * nh].reshape(batch, seq, 2, DN_K_HEADS, 2)
    rows = rows.transpose(0, 3, 2, 4, 1).reshape(batch, DN_K_HEADS, 4, seq)
    return jnp.concatenate([rows, jnp.zeros_like(rows)], axis=2)


def _token_mixer(x2, batch, seq, norm1_w, w_in, conv_w, a_log_f, a_log_b, dt_f, dt_b, dn_norm_w,
                 w_br, w_bd, w_out):
    t = batch * seq
    w_main, w_small = _split_w_in(w_in)
    nw1 = norm1_w.reshape(1, D_MODEL)
    proj = _norm_matmul(x2, nw1, w_main, BF16, tm=min(2048, t), tn=512)

    zeros = jnp.zeros((2 * DN_V_HEADS,), F32)
    pad = jnp.zeros((LANES - 4 * DN_V_HEADS,), F32)
    alog_row = jnp.concatenate([zeros, a_log_f, a_log_b, pad]).reshape(1, LANES)
    dtb_row = jnp.concatenate([zeros, dt_f, dt_b, pad]).reshape(1, LANES)
    gates = _gate_proj(x2, nw1, w_small, alog_row, dtb_row, tm=min(512, t))

    angle = 1.0 / (ROPE_BASE ** jnp.linspace(0.0, 1.0, RET_QK_DIM // 2, dtype=F32))
    theta = jnp.arange(seq, dtype=F32)[:, None] * angle[None, :]
    log_g = jnp.log(1.0 - 2.0 ** (-5.0 - jnp.arange(2 * RET_HEADS, dtype=F32)))
    ret = _retention(proj, jnp.cos(theta), jnp.sin(theta), log_g, batch, seq)

    dn_qkv = _dn_prep(proj, conv_w, seq, tr=min(512, seq), tc=512)
    o_f, o_b = _dn_scan(dn_qkv, gates, _gate_rows(gates, batch, seq), batch, seq)
    dn = _dn_final(o_f, o_b, proj, dn_norm_w.reshape(1, DN_DIM), tr=min(512, t), tc=512)

    merged = _merge(ret, dn, w_br.astype(BF16), w_bd.astype(BF16), proj, tm=min(1024, t), tn=512)
    return _out_proj(merged, w_out.astype(BF16), x2, tm=min(1024, t), tn=512)


def _moe(h, norm2_w, w_router, b_router, w_gate, b_gate, w_up, b_up, w_down, b_down, norm_f_w):
    t = h.shape[0]
    wr = jnp.pad(w_router, ((0, 0), (0, LANES - N_EXPERTS)))
    br = jnp.pad(b_router, (0, LANES - N_EXPERTS)).reshape(1, LANES)
    xn, route, counts = _router(h, norm2_w.reshape(1, D_MODEL), wr, br, tr=min(256, t))

    top_idx = route[:, :TOP_K].astype(jnp.int32)
    rank = route[:, 2 * TOP_K:3 * TOP_K].astype(jnp.int32)
    counts = counts[0, :N_EXPERTS].astype(jnp.int32)
    n_assign = t * TOP_K
    n_rows = ((n_assign + N_EXPERTS * (MOE_BLOCK - 1) + MOE_BLOCK - 1) // MOE_BLOCK) * MOE_BLOCK
    n_blocks = n_rows // MOE_BLOCK
    padded = ((counts + MOE_BLOCK - 1) // MOE_BLOCK) * MOE_BLOCK
    pad_ends = jnp.cumsum(padded)
    pad_starts = pad_ends - padded
    dest = (pad_starts[top_idx] + rank).reshape(-1)
    tok = jnp.repeat(jnp.arange(t, dtype=jnp.int32), TOP_K)
    buf_tok = jnp.zeros((n_rows,), jnp.int32).at[dest].set(tok)
    nb_e = padded // MOE_BLOCK
    blk_start_e = pad_starts // MOE_BLOCK
    nc_e = (nb_e + MOE_CHUNK - 1) // MOE_CHUNK
    c_end_e = jnp.cumsum(nc_e)
    n_chunks = N_EXPERTS + n_blocks // MOE_CHUNK
    cid = jnp.arange(n_chunks, dtype=jnp.int32)
    n_live = c_end_e[-1]
    cid_c = jnp.minimum(cid, n_live - 1)
    e_of = jnp.sum((cid_c[:, None] >= c_end_e[None, :]).astype(jnp.int32), axis=1)
    local = cid_c - (c_end_e - nc_e)[e_of]
    c_start = blk_start_e[e_of] + local * MOE_CHUNK
    c_count = jnp.clip(nb_e[e_of] - local * MOE_CHUNK, 0, MOE_CHUNK)
    c_window = jnp.minimum(c_start, n_blocks - MOE_CHUNK)
    idle = cid >= n_live
    n_used = pad_ends[-1] // MOE_BLOCK
    c_start = jnp.where(idle, n_used + (cid - n_live) * MOE_CHUNK, c_start).astype(jnp.int32)
    c_count = jnp.where(idle, 0, c_count).astype(jnp.int32)

    x_sorted = _dispatch(buf_tok, xn, tr=8 * MOE_BLOCK)
    y_sorted = _experts(e_of.astype(jnp.int32), c_start, c_count, c_window.astype(jnp.int32), x_sorted,
                        w_gate, b_gate.reshape(N_EXPERTS, 1, D_FF),
                        w_up, b_up.reshape(N_EXPERTS, 1, D_FF),
                        w_down, b_down.reshape(N_EXPERTS, 1, D_MODEL), tf=256)
    return _combine(dest, y_sorted, h, route, norm_f_w.reshape(1, D_MODEL), tr=min(512, t))


def kernel(x, norm1_w, w_in, conv_w, dn_a_log_f, dn_a_log_b, dn_dt_bias_f, dn_dt_bias_b, dn_norm_w, w_branch_ret, w_branch_dn, w_out, norm2_w, w_router, b_router, w_gate, b_gate, w_up, b_up, w_down, b_down, norm_f_w):
    batch, seq, d = x.shape
    assert norm1_w.shape[0] == 1, "one layer"
    x2 = x.reshape(batch * seq, d)
    h = _token_mixer(x2, batch, seq, norm1_w[0], w_in[0], conv_w[0], dn_a_log_f[0], dn_a_log_b[0],
                     dn_dt_bias_f[0], dn_dt_bias_b[0], dn_norm_w[0], w_branch_ret[0], w_branch_dn[0], w_out[0])
    out = _moe(h, norm2_w[0], w_router[0], b_router[0], w_gate[0], b_gate[0], w_up[0], b_up[0],
               w_down[0], b_down[0], norm_f_w)
    return out.reshape(batch, seq, d)
```

```python
import functools

import jax
import jax.numpy as jnp
from jax import lax
from jax.experimental import pallas as pl
from jax.experimental.pallas import tpu as pltpu

F32 = jnp.float32
BF16 = jnp.bfloat16

D_MODEL = 2048
RET_HEADS = 4
RET_QK_DIM = 256
RET_V_DIM = 512
ROPE_BASE = 10000.0
DN_K_HEADS = 8
DN_V_HEADS = 16
DN_DIM = 128
DN_CONV = 4
DN_CHUNK = 128
N_EXPERTS = 32
TOP_K = 4
D_FF = 2048
SWIGLU_LIMIT = 7.0
SWIGLU_ALPHA = 1.702
EPS = 1e-6

RET_QK = RET_HEADS * RET_QK_DIM
RET_V = RET_HEADS * RET_V_DIM
DN_QK = DN_K_HEADS * DN_DIM
DN_V = DN_V_HEADS * DN_DIM
DN_CONV_CH = 2 * DN_QK + DN_V

OFF_RQ = 0
OFF_RK = OFF_RQ + RET_QK
OFF_RV = OFF_RK + RET_QK
OFF_RG = OFF_RV + RET_V
OFF_DQKV = OFF_RG + RET_V
OFF_DZ = OFF_DQKV + DN_CONV_CH
OFF_GR = OFF_DZ + DN_V
OFF_GD = OFF_GR + D_MODEL
N_MAIN = OFF_GD + D_MODEL
LANES = 128

RET_CHUNK = 256
DN_TILE = 256
DN_GROUPS_PER_STEP = 2
MOE_BLOCK = 256
MOE_CHUNK = 4
VMEM_LIMIT = 56 * 1024 * 1024
EXPERT_VMEM_LIMIT = 60 * 1024 * 1024


def _cparams(sem):
    return pltpu.CompilerParams(dimension_semantics=sem, vmem_limit_bytes=VMEM_LIMIT)


def _dot(a, b):
    return jnp.dot(a, b, preferred_element_type=F32)


def _dot_nt(a, b):
    return lax.dot_general(a, b, (((1,), (1,)), ((), ())), preferred_element_type=F32)


def _dot_tn(a, b):
    return lax.dot_general(a, b, (((0,), (0,)), ((), ())), preferred_element_type=F32)


def _sigmoid(x):
    return 1.0 / (1.0 + jnp.exp(-x))


def _pack_bf16_pairs(x):
    n = x.shape[1] // 2
    bits = lax.bitcast_convert_type(x.astype(BF16).astype(F32), jnp.uint32)
    return (bits[:, n:] & jnp.uint32(0xFFFF0000)) | (bits[:, :n] >> 16)


def _unpack_bf16_pairs(p):
    lo = lax.bitcast_convert_type(p << 16, F32)
    hi = lax.bitcast_convert_type(p & jnp.uint32(0xFFFF0000), F32)
    return lo, hi


def _norm_matmul_kernel(x_ref, nw_ref, w_ref, o_ref, xn_ref):
    @pl.when(pl.program_id(1) == 0)
    def _():
        x = x_ref[...]
        ms = jnp.mean(x * x, axis=-1, keepdims=True)
        xn_ref[...] = (x * lax.rsqrt(ms + EPS) * nw_ref[...]).astype(BF16)

    o_ref[...] = _dot(xn_ref[...], w_ref[...]).astype(o_ref.dtype)


def _norm_matmul(x, nw, w, out_dtype, tm, tn):
    t, d = x.shape
    n = w.shape[1]
    return pl.pallas_call(
        _norm_matmul_kernel,
        grid=(t // tm, n // tn),
        in_specs=[
            pl.BlockSpec((tm, d), lambda i, j: (i, 0)),
            pl.BlockSpec((1, d), lambda i, j: (0, 0)),
            pl.BlockSpec((d, tn), lambda i, j: (0, j)),
        ],
        out_specs=pl.BlockSpec((tm, tn), lambda i, j: (i, j)),
        out_shape=jax.ShapeDtypeStruct((t, n), out_dtype),
        scratch_shapes=[pltpu.VMEM((tm, d), BF16)],
        compiler_params=_cparams(("parallel", "arbitrary")),
        name="norm_in_proj",
    )(x, nw, w)


def _gate_proj_kernel(x_ref, nw_ref, w_ref, alog_ref, dtb_ref, o_ref):
    x = x_ref[...]
    ms = jnp.mean(x * x, axis=-1, keepdims=True)
    xn = (x * lax.rsqrt(ms + EPS) * nw_ref[...]).astype(BF16)
    p = _dot(xn, w_ref[...])
    lane = lax.broadcasted_iota(jnp.int32, p.shape, 1)
    beta = _sigmoid(p)
    z = p + dtb_ref[...]
    softplus = jnp.maximum(z, 0.0) + jnp.log(1.0 + jnp.exp(-jnp.abs(z)))
    g = -jnp.exp(alog_ref[...]) * softplus
    tm = p.shape[0]
    ci = lax.broadcasted_iota(jnp.int32, (tm, tm), 0)
    cj = lax.broadcasted_iota(jnp.int32, (tm, tm), 1)
    same = _blk(ci, DN_CHUNK) == _blk(cj, DN_CHUNK)

    def chunk_sum(mask):
        return jnp.dot(mask.astype(F32), g, preferred_element_type=F32, precision=lax.Precision.HIGHEST)

    csum_f = chunk_sum(same & (ci >= cj))
    csum_b = chunk_sum(same & (ci <= cj))
    ctot = pltpu.roll(chunk_sum(same), 2 * DN_V_HEADS, 1)
    nh = DN_V_HEADS
    o_ref[...] = jnp.where(lane < 2 * nh, beta,
                           jnp.where(lane < 3 * nh, csum_f,
                                     jnp.where(lane < 4 * nh, csum_b,
                                               jnp.where(lane < 6 * nh, ctot, 0.0))))


def _gate_proj(x, nw, w_small, alog_row, dtb_row, tm):
    t, d = x.shape
    return pl.pallas_call(
        _gate_proj_kernel,
        grid=(t // tm,),
        in_specs=[
            pl.BlockSpec((tm, d), lambda i: (i, 0)),
            pl.BlockSpec((1, d), lambda i: (0, 0)),
            pl.BlockSpec((d, LANES), lambda i: (0, 0)),
            pl.BlockSpec((1, LANES), lambda i: (0, 0)),
            pl.BlockSpec((1, LANES), lambda i: (0, 0)),
        ],
        out_specs=pl.BlockSpec((tm, LANES), lambda i: (i, 0)),
        out_shape=jax.ShapeDtypeStruct((t, LANES), F32),
        compiler_params=_cparams(("parallel",)),
        name="dn_gate_proj",
    )(x, nw, w_small, alog_row, dtb_row)


def _rotary(t, cos, sin):
    t = t.astype(F32)
    half = RET_QK_DIM // 2
    t1 = t[:, :half]
    t2 = t[:, half:]
    return jnp.concatenate([t1 * cos - t2 * sin, t2 * cos + t1 * sin], axis=1)


def _ret_heads(q_ref, k_ref, v_ref, cos_ref, sin_ref):
    cos = cos_ref[...]
    sin = sin_ref[...]
    dk, dv = RET_QK_DIM, RET_V_DIM
    qs = [_rotary(q_ref[:, h * dk:(h + 1) * dk], cos, sin) for h in range(RET_HEADS)]
    ks = [_rotary(k_ref[:, h * dk:(h + 1) * dk], cos, sin) * (dk ** -0.5) for h in range(RET_HEADS)]
    vs = [v_ref[:, h * dv:(h + 1) * dv] for h in range(RET_HEADS)]
    return qs, ks, vs


def _ret_fwd_kernel(lg_ref, q_ref, k_ref, v_ref, cos_ref, sin_ref, o_ref, state_ref, dmask_ref):
    c = RET_CHUNK
    heads = range(RET_HEADS)
    lg_f = [lg_ref[2 * h] for h in heads]
    lg_b = [lg_ref[2 * h + 1] for h in heads]

    @pl.when(pl.program_id(1) == 0)
    def _():
        state_ref[...] = jnp.zeros_like(state_ref)
        i = lax.broadcasted_iota(jnp.int32, (c, c), 0)
        j = lax.broadcasted_iota(jnp.int32, (c, c), 1)
        d = (i - j).astype(F32)
        for h in heads:
            dmask_ref[h] = jnp.where(d >= 0, jnp.exp(d * lg_f[h]), jnp.exp(-d * lg_b[h]))

    qs, ks, vs = _ret_heads(q_ref, k_ref, v_ref, cos_ref, sin_ref)
    idx = lax.broadcasted_iota(jnp.int32, (c, 1), 0).astype(F32)
    scores = [_dot_nt(qs[h].astype(BF16), ks[h].astype(BF16)) * dmask_ref[h] for h in heads]
    states = [state_ref[h] for h in heads]
    inter = [_dot((qs[h] * jnp.exp((idx + 1.0) * lg_f[h])).astype(BF16), states[h].astype(BF16)) for h in heads]
    outs = [_dot(scores[h].astype(BF16), vs[h]) + inter[h] for h in heads]
    for h in heads:
        c_dec = jnp.exp(jnp.full((1, 1), c, F32) * lg_f[h])
        k_dec = jnp.exp((c - 1.0 - idx) * lg_f[h])
        state_ref[h] = states[h] * c_dec + _dot_tn((ks[h] * k_dec).astype(BF16), vs[h])
    o_ref[...] = jnp.concatenate(outs, axis=1)


def _ret_bwd_kernel(lg_ref, q_ref, k_ref, v_ref, cos_ref, sin_ref, oacc_ref, rg_ref, o_ref, state_ref):
    c = RET_CHUNK
    heads = range(RET_HEADS)
    dv = RET_V_DIM
    lg_b = [lg_ref[2 * h + 1] for h in heads]

    @pl.when(pl.program_id(1) == 0)
    def _():
        state_ref[...] = jnp.zeros_like(state_ref)

    qs, ks, vs = _ret_heads(q_ref, k_ref, v_ref, cos_ref, sin_ref)
    idx = lax.broadcasted_iota(jnp.int32, (c, 1), 0).astype(F32)
    states = [state_ref[h] for h in heads]
    inter = [_dot((qs[h] * jnp.exp((c - idx) * lg_b[h])).astype(BF16), states[h].astype(BF16)) for h in heads]
    for h in heads:
        c_dec = jnp.exp(jnp.full((1, 1), c, F32) * lg_b[h])
        k_dec = jnp.exp(idx * lg_b[h])
        state_ref[h] = states[h] * c_dec + _dot_tn((ks[h] * k_dec).astype(BF16), vs[h])
    outs = []
    for h in heads:
        o = oacc_ref[:, h * dv:(h + 1) * dv] + inter[h]
        o = o * lax.rsqrt(jnp.mean(o * o, axis=-1, keepdims=True) + EPS)
        rg = rg_ref[:, h * dv:(h + 1) * dv].astype(F32)
        outs.append((o * (rg * _sigmoid(rg))).astype(o_ref.dtype))
    o_ref[...] = jnp.concatenate(outs, axis=1)


def _retention(proj, cos, sin, log_g, batch, seq):
    c = RET_CHUNK
    nc = seq // c
    t = batch * seq
    half = RET_QK_DIM // 2
    smem = pl.BlockSpec(memory_space=pltpu.SMEM)
    state = pltpu.VMEM((RET_HEADS, RET_QK_DIM, RET_V_DIM), F32)

    def row_f(b, n):
        return b * nc + n

    def row_b(b, n):
        return b * nc + (nc - 1 - n)

    def specs(row):
        return [
            pl.BlockSpec((c, RET_QK), lambda b, n: (row(b, n), OFF_RQ // RET_QK)),
            pl.BlockSpec((c, RET_QK), lambda b, n: (row(b, n), OFF_RK // RET_QK)),
            pl.BlockSpec((c, RET_V), lambda b, n: (row(b, n), OFF_RV // RET_V)),
        ]

    o_acc = pl.pallas_call(
        _ret_fwd_kernel,
        grid=(batch, nc),
        in_specs=[smem] + specs(row_f) + [
            pl.BlockSpec((c, half), lambda b, n: (n, 0)),
            pl.BlockSpec((c, half), lambda b, n: (n, 0)),
        ],
        out_specs=pl.BlockSpec((c, RET_V), lambda b, n: (row_f(b, n), 0)),
        out_shape=jax.ShapeDtypeStruct((t, RET_V), F32),
        scratch_shapes=[state, pltpu.VMEM((RET_HEADS, c, c), F32)],
        compiler_params=_cparams(("parallel", "arbitrary")),
        name="retention_fwd",
    )(log_g, proj, proj, proj, cos, sin)

    return pl.pallas_call(
        _ret_bwd_kernel,
        grid=(batch, nc),
        in_specs=[smem] + specs(row_b) + [
            pl.BlockSpec((c, half), lambda b, n: (nc - 1 - n, 0)),
            pl.BlockSpec((c, half), lambda b, n: (nc - 1 - n, 0)),
            pl.BlockSpec((c, RET_V), lambda b, n: (row_b(b, n), 0)),
            pl.BlockSpec((c, RET_V), lambda b, n: (row_b(b, n), OFF_RG // RET_V)),
        ],
        out_specs=pl.BlockSpec((c, RET_V), lambda b, n: (row_b(b, n), 0)),
        out_shape=jax.ShapeDtypeStruct((t, RET_V), BF16),
        scratch_shapes=[state],
        compiler_params=_cparams(("parallel", "arbitrary")),
        name="retention_bwd",
    )(log_g, proj, proj, proj, cos, sin, o_acc, proj)


DN_HALO = 16


def _dn_prep_kernel(x_ref, prev_ref, next_ref, w_ref, o_ref, *, tiles_per_seq):
    tr = x_ref.shape[0]
    i = pl.program_id(0)
    j = pl.program_id(1)
    pos = i % tiles_per_seq
    has_prev = (pos != 0).astype(F32)
    has_next = (pos != tiles_per_seq - 1).astype(F32)
    x = x_ref[...].astype(F32)
    xc = jnp.concatenate(
        [prev_ref[...].astype(F32) * has_prev, x, next_ref[...].astype(F32) * has_next], axis=0)
    n = tr + 2 * DN_HALO
    w = w_ref[...]
    lo, hi = DN_HALO, DN_HALO + tr
    y = (pltpu.roll(xc, 2, 0)[lo:hi] * w[0:1, :] + pltpu.roll(xc, 1, 0)[lo:hi] * w[1:2, :]
         + x * w[2:3, :] + pltpu.roll(xc, n - 1, 0)[lo:hi] * w[3:4, :])
    y = y * _sigmoid(y)
    tc = y.shape[1]
    qk_tiles = (2 * DN_QK) // tc
    q_tiles = DN_QK // tc
    scale = jnp.where(j < q_tiles, DN_DIM ** -0.5, 1.0).astype(F32)
    is_qk = j < qk_tiles
    outs = []
    for s in range(tc // DN_DIM):
        ys = y[:, s * DN_DIM:(s + 1) * DN_DIM]
        rs = lax.rsqrt(jnp.sum(ys * ys, axis=-1, keepdims=True) + EPS) * scale
        outs.append(ys * jnp.where(is_qk, rs, 1.0))
    o_ref[...] = jnp.concatenate(outs, axis=1).astype(o_ref.dtype)


def _dn_prep(proj, conv_w, seq, tr, tc):
    t = proj.shape[0]
    tiles_per_seq = seq // tr
    nrow = t // tr
    col0 = OFF_DQKV // tc
    hb = tr // DN_HALO
    last_halo = t // DN_HALO - 1
    return pl.pallas_call(
        functools.partial(_dn_prep_kernel, tiles_per_seq=tiles_per_seq),
        grid=(nrow, DN_CONV_CH // tc),
        in_specs=[
            pl.BlockSpec((tr, tc), lambda i, j: (i, col0 + j)),
            pl.BlockSpec((DN_HALO, tc), lambda i, j: (jnp.maximum(i * hb - 1, 0), col0 + j)),
            pl.BlockSpec((DN_HALO, tc), lambda i, j: (jnp.minimum((i + 1) * hb, last_halo), col0 + j)),
            pl.BlockSpec((DN_CONV, tc), lambda i, j: (0, j)),
        ],
        out_specs=pl.BlockSpec((tr, tc), lambda i, j: (i, j)),
        out_shape=jax.ShapeDtypeStruct((t, DN_CONV_CH), BF16),
        compiler_params=_cparams(("parallel", "parallel")),
        name="dn_prep",
    )(proj, proj, proj, conv_w)


def _blk(idx, size):
    return lax.shift_right_logical(idx, size.bit_length() - 1)


def _unit_tri_inverse(ms, ci, cj, chunk):
    eye = (ci == cj).astype(F32)
    base = 8
    diag8 = _blk(ci, base) == _blk(cj, base)
    m8 = [jnp.where(diag8, m, 0.0) for m in ms]
    m8b = [m.astype(BF16) for m in m8]
    p2 = [_dot(m, m) for m in m8b]
    p2b = [p.astype(BF16) for p in p2]
    p4 = [_dot(p, p) for p in p2b]
    inv = [_dot((eye - m).astype(BF16), (eye + p).astype(BF16)) for m, p in zip(m8, p2)]
    inv = [_dot(t.astype(BF16), (eye + p).astype(BF16)) for t, p in zip(inv, p4)]
    s = base
    while s < chunk:
        offmask = (_blk(ci, 2 * s) == _blk(cj, 2 * s)) & (_blk(ci, s) != _blk(cj, s))
        invb = [t.astype(BF16) for t in inv]
        left = [_dot(t, jnp.where(offmask, m, 0.0).astype(BF16)).astype(BF16) for t, m in zip(invb, ms)]
        inv = [t - _dot(lf, tb) for t, lf, tb in zip(inv, left, invb)]
        s *= 2
    return inv


def _dn_scan_kernel(qf_ref, kf_ref, vf_ref, gcf_ref, grf_ref,
                    qb_ref, kb_ref, vb_ref, gcb_ref, grb_ref,
                    of_ref, ob_ref, state_ref):
    r = DN_TILE
    c = DN_CHUNK
    nchunk = r // c
    grp = pl.program_id(1)

    @pl.when(pl.program_id(2) == 0)
    def _():
        state_ref[...] = jnp.zeros_like(state_ref)

    ci = lax.broadcasted_iota(jnp.int32, (r, r), 0)
    cj = lax.broadcasted_iota(jnp.int32, (r, r), 1)
    same = _blk(ci, c) == _blk(cj, c)
    lane = lax.broadcasted_iota(jnp.int32, (r, LANES), 1)
    nh = DN_V_HEADS

    def column(gates, lane_idx):
        return jnp.sum(jnp.where(lane == lane_idx, gates, 0.0), axis=-1, keepdims=True)

    chains = []
    for d, (q_ref, k_ref, v_ref, gc_ref, gr_ref) in enumerate(
            ((qf_ref, kf_ref, vf_ref, gcf_ref, grf_ref), (qb_ref, kb_ref, vb_ref, gcb_ref, grb_ref))):
        rev = d == 1
        before_eq = same & ((ci <= cj) if rev else (ci >= cj))
        strict = same & ((ci < cj) if rev else (ci > cj))
        gates = gc_ref[...]
        for gi in range(DN_GROUPS_PER_STEP):
            q = q_ref[:, gi * DN_DIM:(gi + 1) * DN_DIM]
            k = k_ref[:, gi * DN_DIM:(gi + 1) * DN_DIM]
            kf32 = k.astype(F32)
            qf32 = q.astype(F32)
            kk = _dot_nt(k, k)
            qk = _dot_nt(q, k)
            grow = gr_ref[0, gi]
            for hh in range(2):
                head = 2 * (grp * DN_GROUPS_PER_STEP + gi) + hh
                beta_c = column(gates, d * nh + head)
                gc_c = column(gates, (2 + d) * nh + head)
                gt_c = column(gates, (4 + d) * nh + head)
                gc_r = grow[2 * d + hh:2 * d + hh + 1, :]
                decay = jnp.where(before_eq, jnp.exp(jnp.where(before_eq, gc_c - gc_r, 0.0)), 0.0)
                egc = jnp.exp(gc_c)
                vcol = (2 * gi + hh) * DN_DIM
                v = v_ref[:, vcol:vcol + DN_DIM].astype(F32)
                chains.append(dict(
                    rev=rev, sidx=4 * gi + 2 * d + hh,
                    m=jnp.where(strict, beta_c * kk * decay, 0.0),
                    rhs=jnp.concatenate([v * beta_c, kf32 * (beta_c * egc)], axis=1).astype(BF16),
                    qe=(qf32 * egc).astype(BF16),
                    ke=(kf32 * jnp.exp(gt_c - gc_c)).astype(BF16),
                    qkm=(qk * decay).astype(BF16),
                    egt=jnp.exp(gt_c)))
    tinvs = _unit_tri_inverse([ch["m"] for ch in chains], ci, cj, c)
    for ch, tinv in zip(chains, tinvs):
        uw = _dot(tinv.astype(BF16), ch["rhs"])
        ch["u"] = uw[:, :DN_DIM]
        ch["wb"] = uw[:, DN_DIM:].astype(BF16)
        ch["state"] = state_ref[ch["sidx"]]
        ch["o"] = [None] * nchunk
    for step in range(nchunk):
        for ch in chains:
            cc = nchunk - 1 - step if ch["rev"] else step
            lo, hi = cc * c, (cc + 1) * c
            sb = ch["state"].astype(BF16)
            ws = _dot(jnp.concatenate([ch["wb"][lo:hi], ch["qe"][lo:hi]], axis=0), sb)
            v_new = (ch["u"][lo:hi] - ws[:c]).astype(BF16)
            ch["o"][cc] = ws[c:] + _dot(ch["qkm"][lo:hi, lo:hi], v_new)
            ch["state"] = ch["state"] * ch["egt"][lo:lo + 1, :] + _dot_tn(ch["ke"][lo:hi], v_new)
    for ch in chains:
        state_ref[ch["sidx"]] = ch["state"]
    fwd = sorted((ch for ch in chains if not ch["rev"]), key=lambda ch: ch["sidx"])
    bwd = sorted((ch for ch in chains if ch["rev"]), key=lambda ch: ch["sidx"])
    of_ref[...] = jnp.concatenate([jnp.concatenate(ch["o"], axis=0) for ch in fwd], axis=1).astype(of_ref.dtype)
    ob_ref[...] = jnp.concatenate([jnp.concatenate(ch["o"], axis=0) for ch in bwd], axis=1).astype(ob_ref.dtype)


def _dn_scan(dn_qkv, gates, gates_rows, batch, seq):
    r = DN_TILE
    nt = seq // r
    t = batch * seq
    kcol0 = DN_QK // DN_DIM
    vcol0 = (2 * DN_QK) // (2 * DN_DIM)

    def row_f(b, g, n):
        return b * nt + n

    def row_b(b, g, n):
        return b * nt + (nt - 1 - n)

    ng = DN_GROUPS_PER_STEP

    def specs(row, tile):
        return [
            pl.BlockSpec((r, ng * DN_DIM), lambda b, g, n: (row(b, g, n), g)),
            pl.BlockSpec((r, ng * DN_DIM), lambda b, g, n: (row(b, g, n), kcol0 // ng + g)),
            pl.BlockSpec((r, 2 * ng * DN_DIM), lambda b, g, n: (row(b, g, n), vcol0 // ng + g)),
            pl.BlockSpec((r, LANES), lambda b, g, n: (row(b, g, n), 0)),
            pl.BlockSpec((1, ng, 8, r), lambda b, g, n: (b, g, 0, tile(n))),
        ]

    out_shape = jax.ShapeDtypeStruct((t, DN_V), BF16)
    return pl.pallas_call(
        _dn_scan_kernel,
        grid=(batch, DN_K_HEADS // ng, nt),
        in_specs=specs(row_f, lambda n: n) + specs(row_b, lambda n: nt - 1 - n),
        out_specs=[
            pl.BlockSpec((r, 2 * ng * DN_DIM), lambda b, g, n: (row_f(b, g, n), g)),
            pl.BlockSpec((r, 2 * ng * DN_DIM), lambda b, g, n: (row_b(b, g, n), g)),
        ],
        out_shape=[out_shape, out_shape],
        scratch_shapes=[pltpu.VMEM((4 * ng, DN_DIM, DN_DIM), F32)],
        compiler_params=_cparams(("parallel", "parallel", "arbitrary")),
        name="dn_scan",
    )(dn_qkv, dn_qkv, dn_qkv, gates, gates_rows, dn_qkv, dn_qkv, dn_qkv, gates, gates_rows)


def _dn_final_kernel(of_ref, ob_ref, z_ref, nw_ref, o_ref):
    o = of_ref[...].astype(F32) + ob_ref[...].astype(F32)
    nw = nw_ref[...]
    outs = []
    for s in range(o.shape[1] // DN_DIM):
        os_ = o[:, s * DN_DIM:(s + 1) * DN_DIM]
        outs.append(os_ * lax.rsqrt(jnp.mean(os_ * os_, axis=-1, keepdims=True) + EPS) * nw)
    z = z_ref[...].astype(F32)
    o_ref[...] = (jnp.concatenate(outs, axis=1) * (z * _sigmoid(z))).astype(o_ref.dtype)


def _dn_final(o_f, o_b, proj, norm_w, tr, tc):
    t = o_f.shape[0]
    zcol0 = OFF_DZ // tc
    return pl.pallas_call(
        _dn_final_kernel,
        grid=(t // tr, DN_V // tc),
        in_specs=[
            pl.BlockSpec((tr, tc), lambda i, j: (i, j)),
            pl.BlockSpec((tr, tc), lambda i, j: (i, j)),
            pl.BlockSpec((tr, tc), lambda i, j: (i, zcol0 + j)),
            pl.BlockSpec((1, DN_DIM), lambda i, j: (0, 0)),
        ],
        out_specs=pl.BlockSpec((tr, tc), lambda i, j: (i, j)),
        out_shape=jax.ShapeDtypeStruct((t, DN_V), BF16),
        compiler_params=_cparams(("parallel", "parallel")),
        name="dn_final",
    )(o_f, o_b, proj, norm_w)


def _merge_kernel(ret_ref, dn_ref, wr_ref, wd_ref, gr_ref, gd_ref, o_ref):
    a = _dot(ret_ref[...], wr_ref[...])
    b = _dot(dn_ref[...], wd_ref[...])
    o_ref[...] = (_sigmoid(gr_ref[...].astype(F32)) * a + _sigmoid(gd_ref[...].astype(F32)) * b).astype(o_ref.dtype)


def _merge(ret, dn, w_br, w_bd, proj, tm, tn):
    t = ret.shape[0]
    return pl.pallas_call(
        _merge_kernel,
        grid=(t // tm, D_MODEL // tn),
        in_specs=[
            pl.BlockSpec((tm, RET_V), lambda i, j: (i, 0)),
            pl.BlockSpec((tm, DN_V), lambda i, j: (i, 0)),
            pl.BlockSpec((RET_V, tn), lambda i, j: (0, j)),
            pl.BlockSpec((DN_V, tn), lambda i, j: (0, j)),
            pl.BlockSpec((tm, tn), lambda i, j: (i, OFF_GR // tn + j)),
            pl.BlockSpec((tm, tn), lambda i, j: (i, OFF_GD // tn + j)),
        ],
        out_specs=pl.BlockSpec((tm, tn), lambda i, j: (i, j)),
        out_shape=jax.ShapeDtypeStruct((t, D_MODEL), BF16),
        compiler_params=_cparams(("parallel", "parallel")),
        name="branch_merge",
    )(ret, dn, w_br, w_bd, proj, proj)


def _out_proj_kernel(m_ref, w_ref, x_ref, o_ref):
    o_ref[...] = x_ref[...] + _dot(m_ref[...], w_ref[...])


def _out_proj(merged, w_out, x, tm, tn):
    t = merged.shape[0]
    return pl.pallas_call(
        _out_proj_kernel,
        grid=(t // tm, D_MODEL // tn),
        in_specs=[
            pl.BlockSpec((tm, D_MODEL), lambda i, j: (i, 0)),
            pl.BlockSpec((D_MODEL, tn), lambda i, j: (0, j)),
            pl.BlockSpec((tm, tn), lambda i, j: (i, j)),
        ],
        out_specs=pl.BlockSpec((tm, tn), lambda i, j: (i, j)),
        out_shape=jax.ShapeDtypeStruct((t, D_MODEL), F32),
        compiler_params=_cparams(("parallel", "parallel")),
        name="out_proj",
    )(merged, w_out, x)


def _router_kernel(h_ref, nw_ref, wr_ref, br_ref, xn_ref, route_ref, count_ref, carry_ref):
    tr = h_ref.shape[0]

    @pl.when(pl.program_id(0) == 0)
    def _():
        carry_ref[...] = jnp.zeros_like(carry_ref)

    h = h_ref[...]
    xn = h * lax.rsqrt(jnp.mean(h * h, axis=-1, keepdims=True) + EPS) * nw_ref[...]
    xn_ref[...] = _pack_bf16_pairs(xn)
    logits = jnp.dot(xn, wr_ref[...], preferred_element_type=F32, precision=lax.Precision.HIGHEST) + br_ref[...]
    lane = lax.broadcasted_iota(jnp.int32, (tr, LANES), 1)
    neg = jnp.float32(-jnp.inf)
    work = jnp.where(lane < N_EXPERTS, logits, neg)
    vals, idxs = [], []
    onehot = jnp.zeros((tr, LANES), F32)
    for _ in range(TOP_K):
        mx = jnp.max(work, axis=-1, keepdims=True)
        ix = jnp.min(jnp.where(work == mx, lane, LANES), axis=-1, keepdims=True)
        sel = lane == ix
        onehot = jnp.where(sel, 1.0, onehot)
        work = jnp.where(sel, neg, work)
        vals.append(mx)
        idxs.append(ix)
    exps = [jnp.exp(v - vals[0]) for v in vals]
    denom = exps[0] + exps[1] + exps[2] + exps[3]
    ri = lax.broadcasted_iota(jnp.int32, (tr, tr), 0)
    rj = lax.broadcasted_iota(jnp.int32, (tr, tr), 1)
    lower = (ri > rj).astype(BF16)
    carry = carry_ref[0:1, :]
    rank = carry + _dot(lower, onehot.astype(BF16))
    out = jnp.zeros((tr, LANES), F32)
    for kk in range(TOP_K):
        rk = jnp.sum(jnp.where(lane == idxs[kk], rank, 0.0), axis=-1, keepdims=True)
        out = jnp.where(lane == kk, idxs[kk].astype(F32), out)
        out = jnp.where(lane == TOP_K + kk, exps[kk] / denom, out)
        out = jnp.where(lane == 2 * TOP_K + kk, rk, out)
    route_ref[...] = out
    new_carry = carry + jnp.sum(onehot, axis=0, keepdims=True)
    carry_ref[...] = jnp.broadcast_to(new_carry, carry_ref.shape)
    count_ref[...] = jnp.broadcast_to(new_carry, count_ref.shape)


def _router(h, nw, w_router_pad, b_router_pad, tr):
    t, d = h.shape
    return pl.pallas_call(
        _router_kernel,
        grid=(t // tr,),
        in_specs=[
            pl.BlockSpec((tr, d), lambda i: (i, 0)),
            pl.BlockSpec((1, d), lambda i: (0, 0)),
            pl.BlockSpec((d, LANES), lambda i: (0, 0)),
            pl.BlockSpec((1, LANES), lambda i: (0, 0)),
        ],
        out_specs=[
            pl.BlockSpec((tr, d // 2), lambda i: (i, 0)),
            pl.BlockSpec((tr, LANES), lambda i: (i, 0)),
            pl.BlockSpec((8, LANES), lambda i: (0, 0)),
        ],
        out_shape=[
            jax.ShapeDtypeStruct((t, d // 2), jnp.uint32),
            jax.ShapeDtypeStruct((t, LANES), F32),
            jax.ShapeDtypeStruct((8, LANES), F32),
        ],
        scratch_shapes=[pltpu.VMEM((8, LANES), F32)],
        compiler_params=_cparams(("arbitrary",)),
        name="moe_router",
    )(h, nw, w_router_pad, b_router_pad)


def _row_copy(src_hbm, dst_ref, sem, src_row, dst_row):
    return pltpu.make_async_copy(src_hbm.at[pl.ds(src_row, 1), :], dst_ref.at[pl.ds(dst_row, 1), :], sem)


def _dispatch_kernel(idx_ref, src_hbm, o_ref, sem):
    tr = o_ref.shape[0]

    group = 8

    def start(g, carry):
        for u in range(group):
            r = g * group + u
            _row_copy(src_hbm, o_ref, sem, idx_ref[r], r).start(priority=u % 2)
        return carry

    lax.fori_loop(0, tr // group, start, 0)

    def wait(r, carry):
        _row_copy(src_hbm, o_ref, sem, 0, r).wait()
        return carry

    lax.fori_loop(0, tr, wait, 0, unroll=8)


def _dispatch(buf_tok, src, tr):
    n_rows = buf_tok.shape[0]
    d = src.shape[1]
    return pl.pallas_call(
        _dispatch_kernel,
        grid=(n_rows // tr,),
        in_specs=[
            pl.BlockSpec((tr,), lambda i: (i,), memory_space=pltpu.SMEM),
            pl.BlockSpec(memory_space=pl.ANY),
        ],
        out_specs=pl.BlockSpec((tr, d), lambda i: (i, 0)),
        out_shape=jax.ShapeDtypeStruct((n_rows, d), src.dtype),
        scratch_shapes=[pltpu.SemaphoreType.DMA(())],
        compiler_params=_cparams(("arbitrary",)),
        name="moe_dispatch",
    )(buf_tok, src)


def _block_copy(src_ref, dst_hbm, sem, src_blk, dst_blk):
    return pltpu.make_async_copy(src_ref.at[pl.ds(src_blk * MOE_BLOCK, MOE_BLOCK), :],
                                 dst_hbm.at[pl.ds(pl.multiple_of(dst_blk * MOE_BLOCK, MOE_BLOCK), MOE_BLOCK), :],
                                 sem)


def _expert_kernel(ce_ref, cs_ref, cn_ref, cw_ref, x_ref, wg_ref, bg_ref, wu_ref, bu_ref, wd_ref, bd_ref,
                   y_hbm, xb_ref, acc_ref, ybuf_ref, sem, *, n_blocks):
    i = pl.program_id(0)
    j = pl.program_id(1)
    nf = pl.num_programs(1)
    c = cn_ref[i]
    start = cs_ref[i]
    win_off = (start - cw_ref[i]) * MOE_BLOCK
    last = j == nf - 1

    def run(row0, m):
        rows = pl.ds(row0, m)

        @pl.when(j == 0)
        def _():
            lo, hi = _unpack_bf16_pairs(x_ref[pl.ds(pl.multiple_of(win_off + row0, MOE_BLOCK), m), :])
            xb_ref[rows, :] = jnp.concatenate([lo.astype(BF16), hi.astype(BF16)], axis=1)

        xb = xb_ref[rows, :]
        gate = _dot(xb, wg_ref[...].astype(BF16)) + bg_ref[...]
        up = _dot(xb, wu_ref[...].astype(BF16)) + bu_ref[...]
        gate = jnp.minimum(gate, SWIGLU_LIMIT)
        up = jnp.clip(up, -SWIGLU_LIMIT, SWIGLU_LIMIT)
        hid = (up + 1.0) * gate * _sigmoid(SWIGLU_ALPHA * gate)
        part = _dot(hid.astype(BF16), wd_ref[...].astype(BF16))

        @pl.when(j == 0)
        def _():
            acc_ref[rows, :] = part + bd_ref[...]

        @pl.when((j > 0) & jnp.logical_not(last))
        def _():
            acc_ref[rows, :] += part

        @pl.when(last)
        def _():
            ybuf_ref[rows, :] = _pack_bf16_pairs(acc_ref[rows, :] + part)

    @pl.when(c == 4)
    def _():
        run(0, 4 * MOE_BLOCK)

    @pl.when((c == 2) | (c == 3))
    def _():
        run(0, 2 * MOE_BLOCK)

    @pl.when(c == 3)
    def _():
        run(2 * MOE_BLOCK, MOE_BLOCK)

    @pl.when(c == 1)
    def _():
        run(0, MOE_BLOCK)

    @pl.when(last & (c == 0))
    def _():
        ybuf_ref[...] = jnp.zeros_like(ybuf_ref)

    @pl.when(last)
    def _():
        def writes(r):
            return (r < c) | ((c == 0) & (start + r < n_blocks))

        for r in range(MOE_CHUNK):
            @pl.when(writes(r))
            def _():
                _block_copy(ybuf_ref, y_hbm, sem, r, start + r).start()

        for r in range(MOE_CHUNK):
            @pl.when(writes(r))
            def _():
                _block_copy(ybuf_ref, y_hbm, sem, r, start + r).wait()


def _experts(c_expert, c_start, c_count, c_window, x_sorted, w_gate, b_gate, w_up, b_up, w_down, b_down, tf):
    n_rows, dp = x_sorted.shape
    d = 2 * dp
    nf = D_FF // tf
    n_chunks = c_expert.shape[0]
    rows = MOE_CHUNK * MOE_BLOCK

    def tile(i, j, cn):
        return jnp.where(cn[i] > 0, j, nf - 1)

    grid_spec = pltpu.PrefetchScalarGridSpec(
        num_scalar_prefetch=4,
        grid=(n_chunks, nf),
        in_specs=[
            pl.BlockSpec((pl.Element(rows), pl.Element(dp)),
                         lambda i, j, ce, cs, cn, cw: (cw[i] * MOE_BLOCK, 0)),
            pl.BlockSpec((None, d, tf), lambda i, j, ce, cs, cn, cw: (ce[i], 0, tile(i, j, cn))),
            pl.BlockSpec((None, 1, tf), lambda i, j, ce, cs, cn, cw: (ce[i], 0, tile(i, j, cn))),
            pl.BlockSpec((None, d, tf), lambda i, j, ce, cs, cn, cw: (ce[i], 0, tile(i, j, cn))),
            pl.BlockSpec((None, 1, tf), lambda i, j, ce, cs, cn, cw: (ce[i], 0, tile(i, j, cn))),
            pl.BlockSpec((None, tf, d), lambda i, j, ce, cs, cn, cw: (ce[i], tile(i, j, cn), 0)),
            pl.BlockSpec((None, 1, d), lambda i, j, ce, cs, cn, cw: (ce[i], 0, 0)),
        ],
        out_specs=pl.BlockSpec(memory_space=pl.ANY),
        scratch_shapes=[pltpu.VMEM((rows, d), BF16), pltpu.VMEM((rows, d), F32),
                        pltpu.VMEM((rows, dp), jnp.uint32), pltpu.SemaphoreType.DMA(())],
    )
    return pl.pallas_call(
        functools.partial(_expert_kernel, n_blocks=n_rows // MOE_BLOCK),
        grid_spec=grid_spec,
        out_shape=jax.ShapeDtypeStruct((n_rows, dp), jnp.uint32),
        compiler_params=pltpu.CompilerParams(dimension_semantics=("arbitrary", "arbitrary"),
                                             vmem_limit_bytes=EXPERT_VMEM_LIMIT),
        name="moe_experts",
    )(c_expert, c_start, c_count, c_window, x_sorted, w_gate, b_gate, w_up, b_up, w_down, b_down)


def _combine_kernel(dest_ref, y_hbm, h_ref, route_ref, nw_ref, o_ref, buf_ref, sem):
    tr = h_ref.shape[0]

    def start(r, carry):
        for kk in range(TOP_K):
            _row_copy(y_hbm, buf_ref.at[kk], sem, dest_ref[r * TOP_K + kk], r).start(priority=kk % 2)
        return carry

    lax.fori_loop(0, tr, start, 0, unroll=2)

    def wait(r, carry):
        for kk in range(TOP_K):
            _row_copy(y_hbm, buf_ref.at[kk], sem, 0, r).wait()
        return carry

    lax.fori_loop(0, tr, wait, 0, unroll=2)

    route = route_ref[...]
    h = h_ref[...]
    dp = h.shape[1] // 2
    acc_lo = h[:, :dp]
    acc_hi = h[:, dp:]
    for kk in range(TOP_K):
        w = route[:, TOP_K + kk:TOP_K + kk + 1]
        lo, hi = _unpack_bf16_pairs(buf_ref[kk])
        acc_lo = acc_lo + w * lo
        acc_hi = acc_hi + w * hi
    acc = jnp.concatenate([acc_lo, acc_hi], axis=1)
    o_ref[...] = acc * lax.rsqrt(jnp.mean(acc * acc, axis=-1, keepdims=True) + EPS) * nw_ref[...]


def _combine(dest, y_sorted, h, route, nw, tr):
    t, d = h.shape
    return pl.pallas_call(
        _combine_kernel,
        grid=(t // tr,),
        in_specs=[
            pl.BlockSpec((tr * TOP_K,), lambda i: (i,), memory_space=pltpu.SMEM),
            pl.BlockSpec(memory_space=pl.ANY),
            pl.BlockSpec((tr, d), lambda i: (i, 0)),
            pl.BlockSpec((tr, LANES), lambda i: (i, 0)),
            pl.BlockSpec((1, d), lambda i: (0, 0)),
        ],
        out_specs=pl.BlockSpec((tr, d), lambda i: (i, 0)),
        out_shape=jax.ShapeDtypeStruct((t, d), F32),
        scratch_shapes=[pltpu.VMEM((TOP_K, tr, d // 2), jnp.uint32), pltpu.SemaphoreType.DMA(())],
        compiler_params=_cparams(("arbitrary",)),
        name="moe_combine",
    )(dest, y_sorted, h, route, nw)


def _deinterleave_perm():
    half = RET_QK_DIM // 2
    per_head = jnp.concatenate([jnp.arange(half) * 2, jnp.arange(half) * 2 + 1])
    return (jnp.arange(RET_HEADS)[:, None] * RET_QK_DIM + per_head[None, :]).reshape(-1)


def _split_w_in(w_in):
    o = 0
    parts = {}
    for name, width in (("rq", RET_QK), ("rk", RET_QK), ("rv", RET_V), ("rg", RET_V), ("dqkv", DN_CONV_CH),
                        ("dz", DN_V), ("small", 4 * DN_V_HEADS), ("gr", D_MODEL), ("gd", D_MODEL)):
        parts[name] = w_in[:, o:o + width]
        o += width
    perm = _deinterleave_perm()
    w_main = jnp.concatenate(
        [parts["rq"][:, perm], parts["rk"][:, perm], parts["rv"], parts["rg"], parts["dqkv"], parts["dz"],
         parts["gr"], parts["gd"]], axis=1).astype(BF16)
    w_small = jnp.pad(parts["small"], ((0, 0), (0, LANES - 4 * DN_V_HEADS))).astype(BF16)
    return w_main, w_small


def _gate_rows(gates, batch, seq):
    nh = DN_V_HEADS
    rows = gates[:, 2 * nh:4 * nh].reshape(batch, seq, 2, DN_K_HEADS, 2)
    rows = rows.transpose(0, 3, 2, 4, 1).reshape(batch, DN_K_HEADS, 4, seq)
    return jnp.concatenate([rows, jnp.zeros_like(rows)], axis=2)


def _token_mixer(x2, batch, seq, norm1_w, w_in, conv_w, a_log_f, a_log_b, dt_f, dt_b, dn_norm_w,
                 w_br, w_bd, w_out):
    t = batch * seq
    w_main, w_small = _split_w_in(w_in)
    nw1 = norm1_w.reshape(1, D_MODEL)
    proj = _norm_matmul(x2, nw1, w_main, BF16, tm=min(2048, t), tn=512)

    zeros = jnp.zeros((2 * DN_V_HEADS,), F32)
    pad = jnp.zeros((LANES - 4 * DN_V_HEADS,), F32)
    alog_row = jnp.concatenate([zeros, a_log_f, a_log_b, pad]).reshape(1, LANES)
    dtb_row = jnp.concatenate([zeros, dt_f, dt_b, pad]).reshape(1, LANES)
    gates = _gate_proj(x2, nw1, w_small, alog_row, dtb_row, tm=min(512, t))

    angle = 1.0 / (ROPE_BASE ** jnp.linspace(0.0, 1.0, RET_QK_DIM // 2, dtype=F32))
    theta = jnp.arange(seq, dtype=F32)[:, None] * angle[None, :]
    log_g = jnp.log(1.0 - 2.0 ** (-5.0 - jnp.arange(2 * RET_HEADS, dtype=F32)))
    ret = _retention(proj, jnp.cos(theta), jnp.sin(theta), log_g, batch, seq)

    dn_qkv = _dn_prep(proj, conv_w, seq, tr=min(512, seq), tc=512)
    o_f, o_b = _dn_scan(dn_qkv, gates, _gate_rows(gates, batch, seq), batch, seq)
    dn = _dn_final(o_f, o_b, proj, dn_norm_w.reshape(1, DN_DIM), tr=min(512, t), tc=512)

    merged = _merge(ret, dn, w_br.astype(BF16), w_bd.astype(BF16), proj, tm=min(1024, t), tn=512)
    return _out_proj(merged, w_out.astype(BF16), x2, tm=min(1024, t), tn=512)


def _moe(h, norm2_w, w_router, b_router, w_gate, b_gate, w_up, b_up, w_down, b_down, norm_f_w):
    t = h.shape[0]
    wr = jnp.pad(w_router, ((0, 0), (0, LANES - N_EXPERTS)))
    br = jnp.pad(b_router, (0, LANES - N_EXPERTS)).reshape(1, LANES)
    xn, route, counts = _router(h, norm2_w.reshape(1, D_MODEL), wr, br, tr=min(256, t))

    top_idx = route[:, :TOP_K].astype(jnp.int32)
    rank = route[:, 2 * TOP_K:3 * TOP_K].astype(jnp.int32)
    counts = counts[0, :N_EXPERTS].astype(jnp.int32)
    n_assign = t * TOP_K
    n_rows = ((n_assign + N_EXPERTS * (MOE_BLOCK - 1) + MOE_BLOCK - 1) // MOE_BLOCK) * MOE_BLOCK
    n_blocks = n_rows // MOE_BLOCK
    padded = ((counts + MOE_BLOCK - 1) // MOE_BLOCK) * MOE_BLOCK
    pad_ends = jnp.cumsum(padded)
    pad_starts = pad_ends - padded
    dest = (pad_starts[top_idx] + rank).reshape(-1)
    tok = jnp.repeat(jnp.arange(t, dtype=jnp.int32), TOP_K)
    buf_tok = jnp.zeros((n_rows,), jnp.int32).at[dest].set(tok)
    nb_e = padded // MOE_BLOCK
    blk_start_e = pad_starts // MOE_BLOCK
    nc_e = (nb_e + MOE_CHUNK - 1) // MOE_CHUNK
    c_end_e = jnp.cumsum(nc_e)
    n_chunks = N_EXPERTS + n_blocks // MOE_CHUNK
    cid = jnp.arange(n_chunks, dtype=jnp.int32)
    n_live = c_end_e[-1]
    cid_c = jnp.minimum(cid, n_live - 1)
    e_of = jnp.sum((cid_c[:, None] >= c_end_e[None, :]).astype(jnp.int32), axis=1)
    local = cid_c - (c_end_e - nc_e)[e_of]
    c_start = blk_start_e[e_of] + local * MOE_CHUNK
    c_count = jnp.clip(nb_e[e_of] - local * MOE_CHUNK, 0, MOE_CHUNK)
    c_window = jnp.minimum(c_start, n_blocks - MOE_CHUNK)
    idle = cid >= n_live
    n_used = pad_ends[-1] // MOE_BLOCK
    c_start = jnp.where(idle, n_used + (cid - n_live) * MOE_CHUNK, c_start).astype(jnp.int32)
    c_count = jnp.where(idle, 0, c_count).astype(jnp.int32)

    x_sorted = _dispatch(buf_tok, xn, tr=8 * MOE_BLOCK)
    y_sorted = _experts(e_of.astype(jnp.int32), c_start, c_count, c_window.astype(jnp.int32), x_sorted,
                        w_gate, b_gate.reshape(N_EXPERTS, 1, D_FF),
                        w_up, b_up.reshape(N_EXPERTS, 1, D_FF),
                        w_down, b_down.reshape(N_EXPERTS, 1, D_MODEL), tf=256)
    return _combine(dest, y_sorted, h, route, norm_f_w.reshape(1, D_MODEL), tr=min(512, t))


def kernel(x, norm1_w, w_in, conv_w, dn_a_log_f, dn_a_log_b, dn_dt_bias_f, dn_dt_bias_b, dn_norm_w, w_branch_ret, w_branch_dn, w_out, norm2_w, w_router, b_router, w_gate, b_gate, w_up, b_up, w_down, b_down, norm_f_w):
    batch, seq, d = x.shape
    assert norm1_w.shape[0] == 1, "one layer"
    x2 = x.reshape(batch * seq, d)
    h = _token_mixer(x2, batch, seq, norm1_w[0], w_in[0], conv_w[0], dn_a_log_f[0], dn_a_log_b[0],
                     dn_dt_bias_f[0], dn_dt_bias_b[0], dn_norm_w[0], w_branch_ret[0], w_branch_dn[0], w_out[0])
    out = _moe(h, norm2_w[0], w_router[0], b_router[0], w_gate[0], b_gate[0], w_up[0], b_up[0],
               w_down[0], b_down[0], norm_f_w)
    return out.reshape(batch, seq, d)
```
